```python
import jax, jax.numpy as jnp
from jax import lax
import numpy as np

D_MODEL = 1024
BATCH = 32
SEQ = 2048
DEPTH = 4
DEC_BATCH = 32
DEC_SEQ = 64
PAST_LEN = 2048

CHUNK = 64
CONV_WIDTH = 3
D_CONV = D_MODEL
N_HEADS = 8
QK_NOPE = 128
QK_ROPE = 64
QK_DIM = QK_NOPE + QK_ROPE
V_DIM = D_MODEL // N_HEADS
Q_LORA = 384
KV_LORA = 256
D_FF = 4 * D_MODEL
ROPE_THETA = 10000.0
EPS = 1e-6
Q_BLOCK = 128
SM_SCALE = QK_DIM ** -0.5
NEG_INF = -1e30

OFF_C = D_CONV
OFF_X = 2 * D_CONV
OFF_Q = 3 * D_CONV
OFF_KV = OFF_Q + Q_LORA
OFF_PE = OFF_KV + KV_LORA
OFF_GA = OFF_PE + QK_ROPE
OFF_GB = OFF_GA + D_MODEL
D_IN = OFF_GB + D_MODEL

kernel_name = "hybrid_shortconv_mla_stream_step"


def rmsnorm(x, g):
    xf = x.astype(jnp.float32)
    y = xf * lax.rsqrt(jnp.mean(xf * xf, axis=-1, keepdims=True) + EPS)
    return (y * g.astype(jnp.float32)).astype(x.dtype)


def rope_tables(pos):
    inv = ROPE_THETA ** (-jnp.arange(0, QK_ROPE, 2, dtype=jnp.float32) / QK_ROPE)
    ang = pos.astype(jnp.float32)[:, None] * inv[None, :]
    return jnp.cos(ang), jnp.sin(ang)


def apply_rope(x, cos, sin):
    if x.ndim == 4:
        cos, sin = cos[:, None, :], sin[:, None, :]
    xf = x.astype(jnp.float32)
    x1, x2 = jnp.split(xf, 2, axis=-1)
    out = jnp.concatenate([x1 * cos - x2 * sin, x2 * cos + x1 * sin], axis=-1)
    return out.astype(x.dtype)


def short_conv(u, buf, w):
    L = u.shape[1]
    full = jnp.concatenate([buf, u], axis=1)
    y = sum(w[k][None, None, :] * full[:, k:k + L] for k in range(CONV_WIDTH))
    return y, full[:, -(CONV_WIDTH - 1):]


def mla_attention(q_nope, q_pe, q_pos, ckv, kpe, k_pos, w_uk, w_uv):
    B, Lq = q_nope.shape[0], q_nope.shape[1]
    k_nope = jnp.einsum('bkc,chd->bkhd', ckv, w_uk.reshape(KV_LORA, N_HEADS, QK_NOPE))
    v = jnp.einsum('bkc,chd->bkhd', ckv, w_uv.reshape(KV_LORA, N_HEADS, V_DIM))
    k_chunk = k_pos // CHUNK
    qb = min(Q_BLOCK, Lq)
    nb = Lq // qb
    qn_b = jnp.moveaxis(q_nope.reshape(B, nb, qb, N_HEADS, QK_NOPE), 1, 0)
    qp_b = jnp.moveaxis(q_pe.reshape(B, nb, qb, N_HEADS, QK_ROPE), 1, 0)
    qpos_b = q_pos.reshape(nb, qb)

    def block(args):
        qn, qp, qpos = args
        s = (jnp.einsum('bqhd,bkhd->bhqk', qn, k_nope)
             + jnp.einsum('bqhr,bkr->bhqk', qp, kpe)).astype(jnp.float32) * SM_SCALE
        mask = k_chunk[None, :] <= (qpos // CHUNK)[:, None]
        s = jnp.where(mask[None, None], s, NEG_INF)
        p = jax.nn.softmax(s, axis=-1).astype(v.dtype)
        return jnp.einsum('bhqk,bkhd->bqhd', p, v)

    out = lax.map(block, (qn_b, qp_b, qpos_b))
    return jnp.moveaxis(out, 0, 1).reshape(B, Lq, N_HEADS * V_DIM)


def layer(x, pos, conv_buf, past_ckv, past_kpe, past_pos,
          w_in, conv_w, w_uq, w_uk, w_uv, w_o, w1, w2,
          g_pre_mix, g_post_mix, g_pre_ffn, g_post_ffn, g_q, g_kv):
    B, L, _ = x.shape
    h = rmsnorm(x, g_pre_mix)
    z = h @ w_in
    b_g, c_g, xin, cq, ckv_raw, kpe_raw, ga, gb = jnp.split(
        z, [OFF_C, OFF_X, OFF_Q, OFF_KV, OFF_PE, OFF_GA, OFF_GB], axis=-1)
    yc, new_buf = short_conv(c_g * xin, conv_buf, conv_w)
    y_a = b_g * yc
    cos, sin = rope_tables(pos)
    q = (rmsnorm(cq, g_q) @ w_uq).reshape(B, L, N_HEADS, QK_DIM)
    q_nope, q_pe = q[..., :QK_NOPE], apply_rope(q[..., QK_NOPE:], cos, sin)
    ckv_new = rmsnorm(ckv_raw, g_kv)
    kpe_new = apply_rope(kpe_raw, cos, sin)
    if past_ckv is None:
        ckv_all, kpe_all, k_pos = ckv_new, kpe_new, pos
    else:
        ckv_all = jnp.concatenate([past_ckv, ckv_new], axis=1)
        kpe_all = jnp.concatenate([past_kpe, kpe_new], axis=1)
        k_pos = jnp.concatenate([past_pos, pos])
    y_b = mla_attention(q_nope, q_pe, pos, ckv_all, kpe_all, k_pos, w_uk, w_uv)
    merged = jax.nn.sigmoid(ga) * y_a + jax.nn.sigmoid(gb) * y_b
    x = x + rmsnorm(merged @ w_o, g_post_mix)
    f = jnp.square(jax.nn.relu(rmsnorm(x, g_pre_ffn) @ w1)) @ w2
    x = x + rmsnorm(f, g_post_ffn)
    return x, new_buf, ckv_new, kpe_new


def setup_inputs(seed: int = 0) -> dict:
    key = jax.random.key(seed)
    ks = jax.random.split(key, 24)
    f32 = jnp.float32

    def nrm(k, shape, scale):
        return jax.random.normal(k, shape, f32) * scale

    def gain(k, n):
        return 1.0 + 0.05 * jax.random.normal(k, (DEPTH, n), f32)

    return {
        "x_prompt": nrm(ks[0], (BATCH, SEQ, D_MODEL), 1.0),
        "x_sample": nrm(ks[1], (DEC_BATCH, DEC_SEQ, D_MODEL), 1.0),
        "cache_ckv": nrm(ks[2], (DEPTH, DEC_BATCH, PAST_LEN, KV_LORA), 1.0),
        "cache_kpe": nrm(ks[3], (DEPTH, DEC_BATCH, PAST_LEN, QK_ROPE), 1.0),
        "state_conv": nrm(ks[4], (DEPTH, DEC_BATCH, CONV_WIDTH - 1, D_CONV), 1.0),
        "w_in": nrm(ks[5], (DEPTH, D_MODEL, D_IN), D_MODEL ** -0.5),
        "conv_w": nrm(ks[6], (DEPTH, CONV_WIDTH, D_CONV), CONV_WIDTH ** -0.5),
        "w_uq": nrm(ks[7], (DEPTH, Q_LORA, N_HEADS * QK_DIM), Q_LORA ** -0.5),
        "w_uk": nrm(ks[8], (DEPTH, KV_LORA, N_HEADS * QK_NOPE), KV_LORA ** -0.5),
        "w_uv": nrm(ks[9], (DEPTH, KV_LORA, N_HEADS * V_DIM), KV_LORA ** -0.5),
        "w_o": nrm(ks[10], (DEPTH, D_MODEL, D_MODEL), D_MODEL ** -0.5),
        "w1": nrm(ks[11], (DEPTH, D_MODEL, D_FF), D_MODEL ** -0.5),
        "w2": nrm(ks[12], (DEPTH, D_FF, D_MODEL), D_FF ** -0.5),
        "g_pre_mix": gain(ks[13], D_MODEL),
        "g_post_mix": gain(ks[14], D_MODEL),
        "g_pre_ffn": gain(ks[15], D_MODEL),
        "g_post_ffn": gain(ks[16], D_MODEL),
        "g_q": gain(ks[17], Q_LORA),
        "g_kv": gain(ks[18], KV_LORA),
    }


def reference(x_prompt, x_sample, cache_ckv, cache_kpe, state_conv,
              w_in, conv_w, w_uq, w_uk, w_uv, w_o, w1, w2,
              g_pre_mix, g_post_mix, g_pre_ffn, g_post_ffn, g_q, g_kv):
    B, S, _ = x_prompt.shape
    P = cache_ckv.shape[2]
    Ls = x_sample.shape[1]
    pos_p = jnp.arange(S, dtype=jnp.int32)
    past_pos = jnp.arange(P, dtype=jnp.int32)
    pos_s = P + jnp.arange(Ls, dtype=jnp.int32)
    zero_buf = jnp.zeros((B, CONV_WIDTH - 1, D_CONV), x_prompt.dtype)

    xp, xs = x_prompt, x_sample
    ckv_p, kpe_p, conv_p, ckv_s, kpe_s, conv_s = [], [], [], [], [], []
    for l in range(DEPTH):
        wl = (w_in[l], conv_w[l], w_uq[l], w_uk[l], w_uv[l], w_o[l], w1[l], w2[l],
              g_pre_mix[l], g_post_mix[l], g_pre_ffn[l], g_post_ffn[l], g_q[l], g_kv[l])
        xp, bp, cp, kp = layer(xp, pos_p, zero_buf, None, None, None, *wl)
        xs, bs, cs, ksm = layer(xs, pos_s, state_conv[l], cache_ckv[l], cache_kpe[l],
                                past_pos, *wl)
        ckv_p.append(cp); kpe_p.append(kp); conv_p.append(bp)
        ckv_s.append(cs); kpe_s.append(ksm); conv_s.append(bs)

    return (xp, xs,
            jnp.stack(ckv_p), jnp.stack(kpe_p), jnp.stack(conv_p),
            jnp.stack(ckv_s), jnp.stack(kpe_s), jnp.stack(conv_s))
```

```python
import functools

import numpy as np
import jax
import jax.numpy as jnp
from jax import lax
from jax.experimental import pallas as pl
from jax.experimental.pallas import tpu as pltpu

CHUNK = 64
CONV_WIDTH = 3
N_HEADS = 8
ROPE_THETA = 10000.0
EPS = 1e-6
NEG_INF = -1e30

LANE = 128
HEAD_PAD = 2 * LANE
VMEM_LIMIT_BYTES = 56 * 1024 * 1024

F32 = jnp.float32
BF16 = jnp.bfloat16


def _rms(x, g):
    ms = jnp.mean(x * x, axis=-1, keepdims=True)
    return x * lax.rsqrt(ms + EPS) * g


def _rope(pe, cos_t, sin_a, sin_b):
    return (pe * cos_t + pltpu.roll(pe, LANE // 4, 1) * sin_a
            + pltpu.roll(pe, LANE - LANE // 4, 1) * sin_b)


def _dot(a, b):
    return jnp.dot(a, b, preferred_element_type=F32)


def _dot_nt(a, b):
    return lax.dot_general(a, b, (((1,), (1,)), ((), ())), preferred_element_type=F32)


def _proj_kernel(x_ref, init_ref, cos_ref, sina_ref, sinb_ref,
                 w_ref, wuq_ref, wuk_ref, wuv_ref, convw_ref,
                 gpre_ref, gq_ref, gkv_ref,
                 ya_ref, sgb_ref, q_ref, ckv_ref, kpe_ref, cbuf_ref, *rest,
                 d, seg, nseg, q_lora, kv_lora, rope, sm_scale, expand_kv):
    if expand_kv:
        k_ref, v_ref, carry_ref = rest
    else:
        (carry_ref,) = rest
    inner = pl.program_id(1)
    off_c, off_x, off_q = d, 2 * d, 3 * d
    off_kv = off_q + q_lora
    off_pe = off_kv + kv_lora
    off_ga = off_pe + LANE
    off_gb = off_ga + d

    h = _rms(x_ref[...], gpre_ref[...]).astype(BF16)

    def proj(a, b):
        return _dot(h, w_ref[:, a:b])

    u = proj(off_c, off_x) * proj(off_x, off_q)
    w0 = convw_ref[0:1, :]
    w1 = convw_ref[1:2, :]
    w2 = convw_ref[2:3, :]
    rows = lax.broadcasted_iota(jnp.int32, (seg, 1), 0)

    @pl.when(inner == 0)
    def _():
        carry_ref[...] = init_ref[...]

    yc_parts = []
    for s in range(nseg):
        us = u[s * seg:(s + 1) * seg]
        init = carry_ref[s]
        i0, i1 = init[0:1], init[1:2]
        u1 = jnp.where(rows == 0, i1, pltpu.roll(us, 1, 0))
        u2 = jnp.where(rows == 0, i0, jnp.where(rows == 1, i1, pltpu.roll(us, 2, 0)))
        yc_parts.append(w0 * u2 + w1 * u1 + w2 * us)
        tail = us[seg - 2:seg]
        carry_ref[s] = tail
        cbuf_ref[s] = tail
    yc = yc_parts[0] if nseg == 1 else jnp.concatenate(yc_parts, axis=0)
    ya = jax.nn.sigmoid(proj(off_ga, off_gb)) * proj(0, off_c) * yc
    ya_ref[...] = ya.astype(BF16)
    sgb_ref[...] = jax.nn.sigmoid(proj(off_gb, off_gb + d)).astype(BF16)

    cos_t, sin_a, sin_b = cos_ref[...], sina_ref[...], sinb_ref[...]

    cqn = _rms(proj(off_q, off_kv), gq_ref[...]).astype(BF16)
    for hh in range(N_HEADS):
        qh = _dot(cqn, wuq_ref[:, hh * HEAD_PAD:(hh + 1) * HEAD_PAD]) * sm_scale
        q_ref[:, hh * HEAD_PAD:hh * HEAD_PAD + LANE] = qh[:, :LANE].astype(BF16)
        q_ref[:, hh * HEAD_PAD + LANE:(hh + 1) * HEAD_PAD] = _rope(
            qh[:, LANE:], cos_t, sin_a, sin_b).astype(BF16)

    ckv = _rms(proj(off_kv, off_pe), gkv_ref[...])
    ckv_ref[...] = ckv
    ckvn = ckv.astype(BF16)
    kpe = _rope(proj(off_pe, off_ga), cos_t, sin_a, sin_b)
    kpe_ref[...] = kpe[:, :rope]
    if not expand_kv:
        return
    kpe_b = kpe.astype(BF16)
    for hp in range(N_HEADS // 2):
        kn = _dot(ckvn, wuk_ref[:, hp * HEAD_PAD:(hp + 1) * HEAD_PAD]).astype(BF16)
        base = 2 * hp * HEAD_PAD
        k_ref[:, base:base + LANE] = kn[:, :LANE]
        k_ref[:, base + LANE:base + HEAD_PAD] = kpe_b
        k_ref[:, base + HEAD_PAD:base + HEAD_PAD + LANE] = kn[:, LANE:]
        k_ref[:, base + HEAD_PAD + LANE:base + 2 * HEAD_PAD] = kpe_b
    v_ref[...] = _dot(ckvn, wuv_ref[...]).astype(BF16)


def _proj_call(x2d, conv_init, tabs, wts, l, *, n_outer, n_inner, tm, seg, dims, expand_kv):
    d, q_lora, kv_lora, rope, sm_scale = dims
    nseg = tm // seg
    rows = x2d.shape[0]
    w_all, wuq, wuk, wuv, convw, gpre, gq, gkv = wts
    row_map = lambda o, i: (o * n_inner + i, 0)
    tab_map = lambda o, i: (i, 0)
    seg_map = lambda o, i: (o, 0, 0)
    lay_map = lambda o, i: (l, 0, 0)

    def wspec(a):
        return pl.BlockSpec((None,) + a.shape[1:], lay_map, pipeline_mode=pl.Buffered(1))

    def rspec(n):
        return pl.BlockSpec((tm, n), row_map)

    kern = functools.partial(_proj_kernel, d=d, seg=seg, nseg=nseg, q_lora=q_lora,
                             kv_lora=kv_lora, rope=rope, sm_scale=sm_scale,
                             expand_kv=expand_kv)
    out_shape = [
        jax.ShapeDtypeStruct((rows, d), BF16),
        jax.ShapeDtypeStruct((rows, d), BF16),
        jax.ShapeDtypeStruct((rows, N_HEADS * HEAD_PAD), BF16),
        jax.ShapeDtypeStruct((rows, kv_lora), F32),
        jax.ShapeDtypeStruct((rows, rope), F32),
        jax.ShapeDtypeStruct(conv_init.shape, F32),
    ]
    out_specs = [rspec(d), rspec(d), rspec(N_HEADS * HEAD_PAD), rspec(kv_lora), rspec(rope),
                 pl.BlockSpec((nseg, CONV_WIDTH - 1, d), seg_map)]
    if expand_kv:
        out_shape += [jax.ShapeDtypeStruct((rows, N_HEADS * HEAD_PAD), BF16),
                      jax.ShapeDtypeStruct((rows, N_HEADS * LANE), BF16)]
        out_specs += [rspec(N_HEADS * HEAD_PAD), rspec(N_HEADS * LANE)]
    return pl.pallas_call(
        kern,
        out_shape=out_shape,
        grid=(n_outer, n_inner),
        in_specs=[rspec(d), pl.BlockSpec((nseg, CONV_WIDTH - 1, d), seg_map),
                  pl.BlockSpec((tm, LANE), tab_map), pl.BlockSpec((tm, LANE), tab_map),
                  pl.BlockSpec((tm, LANE), tab_map),
                  wspec(w_all), wspec(wuq), wspec(wuk), wspec(wuv), wspec(convw),
                  wspec(gpre), wspec(gq), wspec(gkv)],
        out_specs=out_specs,
        scratch_shapes=[pltpu.VMEM((nseg, CONV_WIDTH - 1, d), F32)],
        compiler_params=pltpu.CompilerParams(
            dimension_semantics=("arbitrary", "arbitrary"),
            vmem_limit_bytes=VMEM_LIMIT_BYTES),
        name="proj",
    )(x2d, conv_init, *tabs, w_all, wuq, wuk, wuv, convw, gpre, gq, gkv)


def _attn_prompt_kernel(q_ref, k_ref, v_ref, o_ref, m_ref, l_ref, acc_ref, *, tq):
    i = pl.program_id(1)
    m_ref[...] = jnp.full(m_ref.shape, NEG_INF, F32)
    l_ref[...] = jnp.zeros(l_ref.shape, F32)
    acc_ref[...] = jnp.zeros(acc_ref.shape, F32)
    ncol = tq // LANE
    r_chunk = lax.broadcasted_iota(jnp.int32, (tq, LANE), 0) // CHUNK

    def block(j, masked):
        koff = pl.multiple_of(j * tq, tq)
        for hh in range(N_HEADS):
            q = q_ref[:, hh * HEAD_PAD:(hh + 1) * HEAD_PAD]
            k = k_ref[pl.ds(koff, tq), hh * HEAD_PAD:(hh + 1) * HEAD_PAD]
            v = v_ref[pl.ds(koff, tq), hh * LANE:(hh + 1) * LANE]
            s = _dot_nt(q, k)
            cols = []
            for c in range(ncol):
                sc = s[:, c * LANE:(c + 1) * LANE]
                if masked:
                    c_chunk = (lax.broadcasted_iota(jnp.int32, (tq, LANE), 1) + c * LANE) // CHUNK
                    sc = jnp.where(c_chunk <= r_chunk, sc, NEG_INF)
                cols.append(sc)
            m_prev = m_ref[hh]
            m_cur = functools.reduce(jnp.maximum, cols)
            m_next = jnp.maximum(m_prev, jnp.max(m_cur, axis=1, keepdims=True))
            alpha = jnp.exp(m_prev - m_next)
            p_cols = [jnp.exp(sc - m_next) for sc in cols]
            l_ref[hh] = alpha * l_ref[hh] + functools.reduce(jnp.add, p_cols)
            p = jnp.concatenate(p_cols, axis=1).astype(BF16)
            acc_ref[hh] = alpha * acc_ref[hh] + _dot(p, v)
            m_ref[hh] = m_next

    def body(j, carry):
        block(j, False)
        return carry

    lax.fori_loop(0, i, body, 0)
    block(i, True)
    for hh in range(N_HEADS):
        denom = jnp.sum(l_ref[hh], axis=1, keepdims=True)
        o_ref[:, hh * LANE:(hh + 1) * LANE] = (acc_ref[hh] / denom).astype(BF16)


def _attn_prompt_call(q, k, v, *, b, s, tq):
    nq = s // tq
    return pl.pallas_call(
        functools.partial(_attn_prompt_kernel, tq=tq),
        out_shape=jax.ShapeDtypeStruct((b, s, N_HEADS * LANE), BF16),
        grid=(b, nq),
        in_specs=[pl.BlockSpec((None, tq, N_HEADS * HEAD_PAD), lambda bb, i: (bb, i, 0)),
                  pl.BlockSpec((None, s, N_HEADS * HEAD_PAD), lambda bb, i: (bb, 0, 0)),
                  pl.BlockSpec((None, s, N_HEADS * LANE), lambda bb, i: (bb, 0, 0))],
        out_specs=pl.BlockSpec((None, tq, N_HEADS * LANE), lambda bb, i: (bb, i, 0)),
        scratch_shapes=[pltpu.VMEM((N_HEADS, tq, LANE), F32)] * 3,
        compiler_params=pltpu.CompilerParams(
            dimension_semantics=("arbitrary", "arbitrary"),
            vmem_limit_bytes=VMEM_LIMIT_BYTES),
        name="attn_prompt",
    )(q, k, v)


def _attn_sample_kernel(q_ref, pckv_ref, pkpe_ref, nckv_ref, nkpe_ref, wuk_ref, wuv_ref,
                        o_ref, *, ls, rope, new_mask):
    qa, qp = [], []
    for hh in range(N_HEADS):
        qn = q_ref[:, hh * HEAD_PAD:hh * HEAD_PAD + LANE]
        qa.append(_dot_nt(qn, wuk_ref[:, hh * LANE:(hh + 1) * LANE]).astype(BF16))
        qp.append(q_ref[:, hh * HEAD_PAD + LANE:hh * HEAD_PAD + LANE + rope])
    qa = jnp.concatenate(qa, axis=0)
    qp = jnp.concatenate(qp, axis=0)
    pckv = pckv_ref[...].astype(BF16)
    nckv = nckv_ref[...].astype(BF16)
    s_past = _dot_nt(qa, pckv) + _dot_nt(qp, pkpe_ref[...].astype(BF16))
    s_new = _dot_nt(qa, nckv) + _dot_nt(qp, nkpe_ref[...].astype(BF16))
    if new_mask is not None:
        r = lax.broadcasted_iota(jnp.int32, s_new.shape, 0) % ls
        c = lax.broadcasted_iota(jnp.int32, s_new.shape, 1)
        p0 = new_mask
        s_new = jnp.where((p0 + c) // CHUNK <= (p0 + r) // CHUNK, s_new, NEG_INF)
    m = jnp.maximum(jnp.max(s_past, axis=1, keepdims=True),
                    jnp.max(s_new, axis=1, keepdims=True))
    p_past = jnp.exp(s_past - m)
    p_new = jnp.exp(s_new - m)
    denom = jnp.sum(p_past, axis=1, keepdims=True) + jnp.sum(p_new, axis=1, keepdims=True)
    o_lat = (_dot(p_past.astype(BF16), pckv) + _dot(p_new.astype(BF16), nckv)) / denom
    o_lat = o_lat.astype(BF16)
    for hh in range(N_HEADS):
        o_ref[:, hh * LANE:(hh + 1) * LANE] = _dot(
            o_lat[hh * ls:(hh + 1) * ls], wuv_ref[:, hh * LANE:(hh + 1) * LANE]).astype(BF16)


def _attn_sample_call(q, cache_ckv, cache_kpe, nckv, nkpe, wuk, wuv, l, *, b, ls, new_mask):
    p, kv_lora = cache_ckv.shape[2], cache_ckv.shape[3]
    rope = cache_kpe.shape[3]
    bmap = lambda bb: (bb, 0, 0)
    return pl.pallas_call(
        functools.partial(_attn_sample_kernel, ls=ls, rope=rope, new_mask=new_mask),
        out_shape=jax.ShapeDtypeStruct((b, ls, N_HEADS * LANE), BF16),
        grid=(b,),
        in_specs=[pl.BlockSpec((None, ls, N_HEADS * HEAD_PAD), bmap),
                  pl.BlockSpec((None, None, p, kv_lora), lambda bb: (l, bb, 0, 0)),
                  pl.BlockSpec((None, None, p, rope), lambda bb: (l, bb, 0, 0)),
                  pl.BlockSpec((None, ls, kv_lora), bmap),
                  pl.BlockSpec((None, ls, rope), bmap),
                  pl.BlockSpec((None,) + wuk.shape[1:], lambda bb: (l, 0, 0)),
                  pl.BlockSpec((None,) + wuv.shape[1:], lambda bb: (l, 0, 0))],
        out_specs=pl.BlockSpec((None, ls, N_HEADS * LANE), bmap),
        compiler_params=pltpu.CompilerParams(
            dimension_semantics=("arbitrary",),
            vmem_limit_bytes=VMEM_LIMIT_BYTES),
        name="attn_sample",
    )(q, cache_ckv, cache_kpe, nckv, nkpe, wuk, wuv)


def _post_kernel(x_ref, ya_ref, sgb_ref, yb_ref, wo_ref, w1_ref, w2_ref,
                 gpm_ref, gpf_ref, gqf_ref, o_ref, *, d_ff, ff_chunk):
    merged = ya_ref[...].astype(F32) + sgb_ref[...].astype(F32) * yb_ref[...].astype(F32)
    x1 = x_ref[...] + _rms(_dot(merged.astype(BF16), wo_ref[...]), gpm_ref[...])
    h2 = _rms(x1, gpf_ref[...]).astype(BF16)
    acc = jnp.zeros(x1.shape, F32)
    for c in range(d_ff // ff_chunk):
        a = _dot(h2, w1_ref[:, c * ff_chunk:(c + 1) * ff_chunk])
        a = jnp.square(jnp.maximum(a, 0.0)).astype(BF16)
        acc = acc + _dot(a, w2_ref[c * ff_chunk:(c + 1) * ff_chunk, :])
    o_ref[...] = x1 + _rms(acc, gqf_ref[...])


def _post_call(x2d, ya, sgb, yb, wts, l, *, tm):
    rows, d = x2d.shape
    wo, w1, w2, gpm, gpf, gqf = wts
    d_ff = w1.shape[-1]
    row_map = lambda i: (i, 0)
    lay_map = lambda i: (l, 0, 0)

    def wspec(a):
        return pl.BlockSpec((None,) + a.shape[1:], lay_map, pipeline_mode=pl.Buffered(1))

    rspec = pl.BlockSpec((tm, d), row_map)
    return pl.pallas_call(
        functools.partial(_post_kernel, d_ff=d_ff, ff_chunk=min(d_ff, 1024)),
        out_shape=jax.ShapeDtypeStruct((rows, d), F32),
        grid=(rows // tm,),
        in_specs=[rspec, rspec, rspec, rspec, wspec(wo), wspec(w1), wspec(w2),
                  wspec(gpm), wspec(gpf), wspec(gqf)],
        out_specs=rspec,
        compiler_params=pltpu.CompilerParams(
            dimension_semantics=("arbitrary",),
            vmem_limit_bytes=VMEM_LIMIT_BYTES),
        name="post",
    )(x2d, ya, sgb, yb, wo, w1, w2, gpm, gpf, gqf)


def _rope_tables(pos, rope):
    half = rope // 2
    inv = ROPE_THETA ** (-jnp.arange(0, rope, 2, dtype=F32) / rope)
    ang = pos.astype(F32)[:, None] * inv[None, :]
    cos, sin = jnp.cos(ang), jnp.sin(ang)
    z = jnp.zeros_like(cos)
    pad = jnp.zeros((pos.shape[0], LANE - rope), F32)
    cos_t = jnp.concatenate([cos, cos, pad], axis=1)
    sin_a = jnp.concatenate([z, sin, pad], axis=1)
    sin_b = jnp.concatenate([-sin, z, pad], axis=1)
    del half
    return cos_t, sin_a, sin_b


def _tile(n, pref):
    t = min(n, pref)
    assert n % t == 0, (n, t)
    return t


def kernel(x_prompt, x_sample, cache_ckv, cache_kpe, state_conv, w_in, conv_w, w_uq, w_uk, w_uv,
           w_o, w1, w2, g_pre_mix, g_post_mix, g_pre_ffn, g_post_ffn, g_q, g_kv):
    b, s, d = x_prompt.shape
    bs, ls, _ = x_sample.shape
    depth, _, p, kv_lora = cache_ckv.shape
    rope = cache_kpe.shape[-1]
    q_lora = g_q.shape[-1]
    nope = w_uk.shape[-1] // N_HEADS
    v_dim = w_uv.shape[-1] // N_HEADS
    assert nope == LANE and v_dim == LANE and rope * 2 == LANE and d == N_HEADS * v_dim
    assert conv_w.shape[1] == CONV_WIDTH and s % CHUNK == 0 and ls >= CONV_WIDTH - 1
    sm_scale = float(nope + rope) ** -0.5
    dims = (d, q_lora, kv_lora, rope, sm_scale)

    off_pe_end = 3 * d + q_lora + kv_lora + rope
    w_all = jnp.concatenate(
        [w_in[:, :, :off_pe_end], jnp.zeros((depth, d, LANE - rope), w_in.dtype),
         w_in[:, :, off_pe_end:]], axis=2).astype(BF16)
    wuq = jnp.pad(w_uq.reshape(depth, q_lora, N_HEADS, nope + rope),
                  ((0, 0), (0, 0), (0, 0), (0, HEAD_PAD - nope - rope))
                  ).reshape(depth, q_lora, N_HEADS * HEAD_PAD).astype(BF16)
    wuk, wuv = w_uk.astype(BF16), w_uv.astype(BF16)
    wo_b, w1_b, w2_b = w_o.astype(BF16), w1.astype(BF16), w2.astype(BF16)
    g3 = lambda g: g[:, None, :]
    proj_w = (w_all, wuq, wuk, wuv, conv_w, g3(g_pre_mix), g3(g_q), g3(g_kv))
    post_w = (wo_b, w1_b, w2_b, g3(g_post_mix), g3(g_pre_ffn), g3(g_post_ffn))

    tm_p = _tile(s, 512)
    tq = _tile(s, 512)
    segs_per_tile = max(1, min(bs, 512 // ls))
    assert bs % segs_per_tile == 0
    tm_s = segs_per_tile * ls
    tabs_p = _rope_tables(jnp.arange(s, dtype=jnp.int32), rope)
    tabs_s = tuple(jnp.tile(t, (segs_per_tile, 1))
                   for t in _rope_tables(p + jnp.arange(ls, dtype=jnp.int32), rope))
    pos_new = p + np.arange(ls)
    new_all_visible = bool(np.all((pos_new[None, :] // CHUNK) <= (pos_new[:, None] // CHUNK)))
    new_mask = None if new_all_visible else p

    xp = x_prompt.reshape(b * s, d)
    xs = x_sample.reshape(bs * ls, d)
    zero_buf = jnp.zeros((b, CONV_WIDTH - 1, d), F32)
    ckv_p, kpe_p, conv_p, ckv_s, kpe_s, conv_s = [], [], [], [], [], []
    for l in range(depth):
        ya, sgb, q, ckv, kpe, cbuf, k, v = _proj_call(
            xp, zero_buf, tabs_p, proj_w, l, n_outer=b, n_inner=s // tm_p, tm=tm_p, seg=tm_p,
            dims=dims, expand_kv=True)
        yb = _attn_prompt_call(q.reshape(b, s, -1), k.reshape(b, s, -1), v.reshape(b, s, -1),
                               b=b, s=s, tq=tq)
        xp = _post_call(xp, ya, sgb, yb.reshape(b * s, d), post_w, l, tm=tm_p)
        ckv_p.append(ckv.reshape(b, s, kv_lora))
        kpe_p.append(kpe.reshape(b, s, rope))
        conv_p.append(cbuf)

        ya, sgb, q, ckv, kpe, cbuf = _proj_call(
            xs, state_conv[l], tabs_s, proj_w, l, n_outer=bs // segs_per_tile, n_inner=1,
            tm=tm_s, seg=ls, dims=dims, expand_kv=False)
        ckv3, kpe3 = ckv.reshape(bs, ls, kv_lora), kpe.reshape(bs, ls, rope)
        yb = _attn_sample_call(q.reshape(bs, ls, -1), cache_ckv, cache_kpe, ckv3, kpe3,
                               wuk, wuv, l, b=bs, ls=ls, new_mask=new_mask)
        xs = _post_call(xs, ya, sgb, yb.reshape(bs * ls, d), post_w, l, tm=tm_s)
        ckv_s.append(ckv3)
        kpe_s.append(kpe3)
        conv_s.append(cbuf)

    return (xp.reshape(b, s, d), xs.reshape(bs, ls, d),
            jnp.stack(ckv_p), jnp.stack(kpe_p), jnp.stack(conv_p),
            jnp.stack(ckv_s), jnp.stack(kpe_s), jnp.stack(conv_s))
```

```python
import functools

import numpy as np
import jax
import jax.numpy as jnp
from jax import lax
from jax.experimental import pallas as pl
from jax.experimental.pallas import tpu as pltpu

CHUNK = 64
CONV_WIDTH = 3
N_HEADS = 8
ROPE_THETA = 10000.0
EPS = 1e-6
NEG_INF = -1e30

LANE = 128
HEAD_PAD = 2 * LANE
VMEM_LIMIT_BYTES = 56 * 1024 * 1024

F32 = jnp.float32
BF16 = jnp.bfloat16


def _rms(x, g):
    ms = jnp.mean(x * x, axis=-1, keepdims=True)
    return x * lax.rsqrt(ms + EPS) * g


def _rope(pe, cos_t, sin_a, sin_b):
    return (pe * cos_t + pltpu.roll(pe, LANE // 4, 1) * sin_a
            + pltpu.roll(pe, LANE - LANE // 4, 1) * sin_b)


def _dot(a, b):
    return jnp.dot(a, b, preferred_element_type=F32)


def _dot_nt(a, b):
    return lax.dot_general(a, b, (((1,), (1,)), ((), ())), preferred_element_type=F32)


def _proj_kernel(x_ref, init_ref, cos_ref, sina_ref, sinb_ref,
                 w_ref, wuq_ref, wuk_ref, wuv_ref, convw_ref,
                 gpre_ref, gq_ref, gkv_ref,
                 ya_ref, sgb_ref, q_ref, ckv_ref, kpe_ref, cbuf_ref, *rest,
                 d, seg, nseg, q_lora, kv_lora, rope, sm_scale, expand_kv):
    if expand_kv:
        k_ref, v_ref, carry_ref = rest
    else:
        (carry_ref,) = rest
    inner = pl.program_id(1)
    off_c, off_x, off_q = d, 2 * d, 3 * d
    off_kv = off_q + q_lora
    off_pe = off_kv + kv_lora
    off_ga = off_pe + LANE
    off_gb = off_ga + d

    h = _rms(x_ref[...], gpre_ref[...]).astype(BF16)

    def proj(a, b):
        return _dot(h, w_ref[:, a:b])

    u = proj(off_c, off_x) * proj(off_x, off_q)
    w0 = convw_ref[0:1, :]
    w1 = convw_ref[1:2, :]
    w2 = convw_ref[2:3, :]
    rows = lax.broadcasted_iota(jnp.int32, (seg, 1), 0)

    @pl.when(inner == 0)
    def _():
        carry_ref[...] = init_ref[...]

    yc_parts = []
    for s in range(nseg):
        us = u[s * seg:(s + 1) * seg]
        init = carry_ref[s]
        i0, i1 = init[0:1], init[1:2]
        u1 = jnp.where(rows == 0, i1, pltpu.roll(us, 1, 0))
        u2 = jnp.where(rows == 0, i0, jnp.where(rows == 1, i1, pltpu.roll(us, 2, 0)))
        yc_parts.append(w0 * u2 + w1 * u1 + w2 * us)
        tail = us[seg - 2:seg]
        carry_ref[s] = tail
        cbuf_ref[s] = tail
    yc = yc_parts[0] if nseg == 1 else jnp.concatenate(yc_parts, axis=0)
    ya = jax.nn.sigmoid(proj(off_ga, off_gb)) * proj(0, off_c) * yc
    ya_ref[...] = ya.astype(BF16)
    sgb_ref[...] = jax.nn.sigmoid(proj(off_gb, off_gb + d)).astype(BF16)

    cos_t, sin_a, sin_b = cos_ref[...], sina_ref[...], sinb_ref[...]

    cqn = _rms(proj(off_q, off_kv), gq_ref[...]).astype(BF16)
    for hh in range(N_HEADS):
        qh = _dot(cqn, wuq_ref[:, hh * HEAD_PAD:(hh + 1) * HEAD_PAD]) * sm_scale
        q_ref[:, hh * HEAD_PAD:hh * HEAD_PAD + LANE] = qh[:, :LANE].astype(BF16)
        q_ref[:, hh * HEAD_PAD + LANE:(hh + 1) * HEAD_PAD] = _rope(
            qh[:, LANE:], cos_t, sin_a, sin_b).astype(BF16)

    ckv = _rms(proj(off_kv, off_pe), gkv_ref[...])
    ckv_ref[...] = ckv
    ckvn = ckv.astype(BF16)
    kpe = _rope(proj(off_pe, off_ga), cos_t, sin_a, sin_b)
    kpe_ref[...] = kpe[:, :rope]
    if not expand_kv:
        return
    kpe_b = kpe.astype(BF16)
    for hp in range(N_HEADS // 2):
        kn = _dot(ckvn, wuk_ref[:, hp * HEAD_PAD:(hp + 1) * HEAD_PAD]).astype(BF16)
        base = 2 * hp * HEAD_PAD
        k_ref[:, base:base + LANE] = kn[:, :LANE]
        k_ref[:, base + LANE:base + HEAD_PAD] = kpe_b
        k_ref[:, base + HEAD_PAD:base + HEAD_PAD + LANE] = kn[:, LANE:]
        k_ref[:, base + HEAD_PAD + LANE:base + 2 * HEAD_PAD] = kpe_b
    v_ref[...] = _dot(ckvn, wuv_ref[...]).astype(BF16)


def _proj_call(x2d, conv_init, tabs, wts, l, *, n_outer, n_inner, tm, seg, dims, expand_kv):
    d, q_lora, kv_lora, rope, sm_scale = dims
    nseg = tm // seg
    rows = x2d.shape[0]
    w_all, wuq, wuk, wuv, convw, gpre, gq, gkv = wts
    row_map = lambda o, i: (o * n_inner + i, 0)
    tab_map = lambda o, i: (i, 0)
    seg_map = lambda o, i: (o, 0, 0)
    lay_map = lambda o, i: (l, 0, 0)

    def wspec(a):
        return pl.BlockSpec((None,) + a.shape[1:], lay_map, pipeline_mode=pl.Buffered(1))

    def rspec(n):
        return pl.BlockSpec((tm, n), row_map)

    kern = functools.partial(_proj_kernel, d=d, seg=seg, nseg=nseg, q_lora=q_lora,
                             kv_lora=kv_lora, rope=rope, sm_scale=sm_scale,
                             expand_kv=expand_kv)
    out_shape = [
        jax.ShapeDtypeStruct((rows, d), BF16),
        jax.ShapeDtypeStruct((rows, d), BF16),
        jax.ShapeDtypeStruct((rows, N_HEADS * HEAD_PAD), BF16),
        jax.ShapeDtypeStruct((rows, kv_lora), F32),
        jax.ShapeDtypeStruct((rows, rope), F32),
        jax.ShapeDtypeStruct(conv_init.shape, F32),
    ]
    out_specs = [rspec(d), rspec(d), rspec(N_HEADS * HEAD_PAD), rspec(kv_lora), rspec(rope),
                 pl.BlockSpec((nseg, CONV_WIDTH - 1, d), seg_map)]
    if expand_kv:
        out_shape += [jax.ShapeDtypeStruct((rows, N_HEADS * HEAD_PAD), BF16),
                      jax.ShapeDtypeStruct((rows, N_HEADS * LANE), BF16)]
        out_specs += [rspec(N_HEADS * HEAD_PAD), rspec(N_HEADS * LANE)]
    return pl.pallas_call(
        kern,
        out_shape=out_shape,
        grid=(n_outer, n_inner),
        in_specs=[rspec(d), pl.BlockSpec((nseg, CONV_WIDTH - 1, d), seg_map),
                  pl.BlockSpec((tm, LANE), tab_map), pl.BlockSpec((tm, LANE), tab_map),
                  pl.BlockSpec((tm, LANE), tab_map),
                  wspec(w_all), wspec(wuq), wspec(wuk), wspec(wuv), wspec(convw),
                  wspec(gpre), wspec(gq), wspec(gkv)],
        out_specs=out_specs,
        scratch_shapes=[pltpu.VMEM((nseg, CONV_WIDTH - 1, d), F32)],
        compiler_params=pltpu.CompilerParams(
            dimension_semantics=("arbitrary", "arbitrary"),
            vmem_limit_bytes=VMEM_LIMIT_BYTES),
        name="proj",
    )(x2d, conv_init, *tabs, w_all, wuq, wuk, wuv, convw, gpre, gq, gkv)


def _attn_prompt_kernel(q_ref, k_ref, v_ref, o_ref, m_ref, acc_ref, *, tq, nq):
    i = pl.program_id(1)
    m_ref[...] = jnp.full(m_ref.shape, NEG_INF, F32)
    acc_ref[...] = jnp.zeros(acc_ref.shape, F32)
    ncol = tq // LANE
    r_chunk = lax.broadcasted_iota(jnp.int32, (tq, LANE), 0) // CHUNK
    ones = jnp.ones((tq, LANE), BF16)

    def block(j, masked):
        koff = j * tq
        for hh in range(N_HEADS):
            q = q_ref[:, hh * HEAD_PAD:(hh + 1) * HEAD_PAD]
            k = k_ref[koff:koff + tq, hh * HEAD_PAD:(hh + 1) * HEAD_PAD]
            v = v_ref[koff:koff + tq, hh * LANE:(hh + 1) * LANE]
            s = _dot_nt(q, k)
            cols = []
            for c in range(ncol):
                sc = s[:, c * LANE:(c + 1) * LANE]
                if masked:
                    c_chunk = (lax.broadcasted_iota(jnp.int32, (tq, LANE), 1) + c * LANE) // CHUNK
                    sc = jnp.where(c_chunk <= r_chunk, sc, NEG_INF)
                cols.append(sc)
            m_prev = m_ref[hh]
            m_cur = functools.reduce(jnp.maximum, cols)
            m_next = jnp.maximum(m_prev, jnp.max(m_cur, axis=1, keepdims=True))
            alpha = jnp.exp2(m_prev - m_next)
            p = jnp.concatenate([jnp.exp2(sc - m_next) for sc in cols], axis=1).astype(BF16)
            pv = _dot(p, jnp.concatenate([v, ones], axis=1))
            acc_ref[hh] = jnp.concatenate([alpha, alpha], axis=1) * acc_ref[hh] + pv
            m_ref[hh] = m_next

    for qi in range(nq):
        @pl.when(i == qi)
        def _(qi=qi):
            for j in range(qi):
                block(j, False)
            block(qi, True)

    for hh in range(N_HEADS):
        acc = acc_ref[hh]
        o_ref[:, hh * LANE:(hh + 1) * LANE] = (acc[:, :LANE] / acc[:, LANE:]).astype(BF16)


def _attn_prompt_call(q, k, v, *, b, s, tq):
    nq = s // tq
    return pl.pallas_call(
        functools.partial(_attn_prompt_kernel, tq=tq, nq=nq),
        out_shape=jax.ShapeDtypeStruct((b, s, N_HEADS * LANE), BF16),
        grid=(b, nq),
        in_specs=[pl.BlockSpec((None, tq, N_HEADS * HEAD_PAD), lambda bb, i: (bb, i, 0)),
                  pl.BlockSpec((None, s, N_HEADS * HEAD_PAD), lambda bb, i: (bb, 0, 0)),
                  pl.BlockSpec((None, s, N_HEADS * LANE), lambda bb, i: (bb, 0, 0))],
        out_specs=pl.BlockSpec((None, tq, N_HEADS * LANE), lambda bb, i: (bb, i, 0)),
        scratch_shapes=[pltpu.VMEM((N_HEADS, tq, LANE), F32),
                        pltpu.VMEM((N_HEADS, tq, 2 * LANE), F32)],
        compiler_params=pltpu.CompilerParams(
            dimension_semantics=("arbitrary", "arbitrary"),
            vmem_limit_bytes=VMEM_LIMIT_BYTES),
        name="attn_prompt",
    )(q, k, v)


def _attn_sample_kernel(q_ref, pckv_ref, pkpe_ref, nckv_ref, nkpe_ref, wuk_ref, wuv_ref,
                        o_ref, *, ls, rope, new_mask):
    qa, qp = [], []
    for hh in range(N_HEADS):
        qn = q_ref[:, hh * HEAD_PAD:hh * HEAD_PAD + LANE]
        qa.append(_dot_nt(qn, wuk_ref[:, hh * LANE:(hh + 1) * LANE]).astype(BF16))
        qp.append(q_ref[:, hh * HEAD_PAD + LANE:hh * HEAD_PAD + LANE + rope])
    qa = jnp.concatenate(qa, axis=0)
    qp = jnp.concatenate(qp, axis=0)
    pckv = pckv_ref[...].astype(BF16)
    nckv = nckv_ref[...].astype(BF16)
    s_past = _dot_nt(qa, pckv) + _dot_nt(qp, pkpe_ref[...].astype(BF16))
    s_new = _dot_nt(qa, nckv) + _dot_nt(qp, nkpe_ref[...].astype(BF16))
    if new_mask is not None:
        r = lax.broadcasted_iota(jnp.int32, s_new.shape, 0) % ls
        c = lax.broadcasted_iota(jnp.int32, s_new.shape, 1)
        p0 = new_mask
        s_new = jnp.where((p0 + c) // CHUNK <= (p0 + r) // CHUNK, s_new, NEG_INF)
    m = jnp.maximum(jnp.max(s_past, axis=1, keepdims=True),
                    jnp.max(s_new, axis=1, keepdims=True))
    p_past = jnp.exp2(s_past - m)
    p_new = jnp.exp2(s_new - m)
    denom = jnp.sum(p_past, axis=1, keepdims=True) + jnp.sum(p_new, axis=1, keepdims=True)
    o_lat = (_dot(p_past.astype(BF16), pckv) + _dot(p_new.astype(BF16), nckv)) / denom
    o_lat = o_lat.astype(BF16)
    for hh in range(N_HEADS):
        o_ref[:, hh * LANE:(hh + 1) * LANE] = _dot(
            o_lat[hh * ls:(hh + 1) * ls], wuv_ref[:, hh * LANE:(hh + 1) * LANE]).astype(BF16)


def _attn_sample_call(q, cache_ckv, cache_kpe, nckv, nkpe, wuk, wuv, l, *, b, ls, new_mask):
    p, kv_lora = cache_ckv.shape[2], cache_ckv.shape[3]
    rope = cache_kpe.shape[3]
    bmap = lambda bb: (bb, 0, 0)
    return pl.pallas_call(
        functools.partial(_attn_sample_kernel, ls=ls, rope=rope, new_mask=new_mask),
        out_shape=jax.ShapeDtypeStruct((b, ls, N_HEADS * LANE), BF16),
        grid=(b,),
        in_specs=[pl.BlockSpec((None, ls, N_HEADS * HEAD_PAD), bmap),
                  pl.BlockSpec((None, None, p, kv_lora), lambda bb: (l, bb, 0, 0)),
                  pl.BlockSpec((None, None, p, rope), lambda bb: (l, bb, 0, 0)),
                  pl.BlockSpec((None, ls, kv_lora), bmap),
                  pl.BlockSpec((None, ls, rope), bmap),
                  pl.BlockSpec((None,) + wuk.shape[1:], lambda bb: (l, 0, 0)),
                  pl.BlockSpec((None,) + wuv.shape[1:], lambda bb: (l, 0, 0))],
        out_specs=pl.BlockSpec((None, ls, N_HEADS * LANE), bmap),
        compiler_params=pltpu.CompilerParams(
            dimension_semantics=("arbitrary",),
            vmem_limit_bytes=VMEM_LIMIT_BYTES),
        name="attn_sample",
    )(q, cache_ckv, cache_kpe, nckv, nkpe, wuk, wuv)


def _post_kernel(x_ref, ya_ref, sgb_ref, yb_ref, wo_ref, w1_ref, w2_ref,
                 gpm_ref, gpf_ref, gqf_ref, o_ref, *, d_ff, ff_chunk):
    merged = ya_ref[...].astype(F32) + sgb_ref[...].astype(F32) * yb_ref[...].astype(F32)
    x1 = x_ref[...] + _rms(_dot(merged.astype(BF16), wo_ref[...]), gpm_ref[...])
    h2 = _rms(x1, gpf_ref[...]).astype(BF16)
    acc = jnp.zeros(x1.shape, F32)
    for c in range(d_ff // ff_chunk):
        a = _dot(h2, w1_ref[:, c * ff_chunk:(c + 1) * ff_chunk])
        a = jnp.square(jnp.maximum(a, 0.0)).astype(BF16)
        acc = acc + _dot(a, w2_ref[c * ff_chunk:(c + 1) * ff_chunk, :])
    o_ref[...] = x1 + _rms(acc, gqf_ref[...])


def _post_call(x2d, ya, sgb, yb, wts, l, *, tm):
    rows, d = x2d.shape
    wo, w1, w2, gpm, gpf, gqf = wts
    d_ff = w1.shape[-1]
    row_map = lambda i: (i, 0)
    lay_map = lambda i: (l, 0, 0)

    def wspec(a):
        return pl.BlockSpec((None,) + a.shape[1:], lay_map, pipeline_mode=pl.Buffered(1))

    rspec = pl.BlockSpec((tm, d), row_map)
    return pl.pallas_call(
        functools.partial(_post_kernel, d_ff=d_ff, ff_chunk=min(d_ff, 1024)),
        out_shape=jax.ShapeDtypeStruct((rows, d), F32),
        grid=(rows // tm,),
        in_specs=[rspec, rspec, rspec, rspec, wspec(wo), wspec(w1), wspec(w2),
                  wspec(gpm), wspec(gpf), wspec(gqf)],
        out_specs=rspec,
        compiler_params=pltpu.CompilerParams(
            dimension_semantics=("arbitrary",),
            vmem_limit_bytes=VMEM_LIMIT_BYTES),
        name="post",
    )(x2d, ya, sgb, yb, wo, w1, w2, gpm, gpf, gqf)


def _rope_tables(pos, rope):
    half = rope // 2
    inv = ROPE_THETA ** (-jnp.arange(0, rope, 2, dtype=F32) / rope)
    ang = pos.astype(F32)[:, None] * inv[None, :]
    cos, sin = jnp.cos(ang), jnp.sin(ang)
    z = jnp.zeros_like(cos)
    pad = jnp.zeros((pos.shape[0], LANE - rope), F32)
    cos_t = jnp.concatenate([cos, cos, pad], axis=1)
    sin_a = jnp.concatenate([z, sin, pad], axis=1)
    sin_b = jnp.concatenate([-sin, z, pad], axis=1)
    del half
    return cos_t, sin_a, sin_b


def _tile(n, pref):
    t = min(n, pref)
    assert n % t == 0, (n, t)
    return t


def kernel(x_prompt, x_sample, cache_ckv, cache_kpe, state_conv, w_in, conv_w, w_uq, w_uk, w_uv,
           w_o, w1, w2, g_pre_mix, g_post_mix, g_pre_ffn, g_post_ffn, g_q, g_kv):
    b, s, d = x_prompt.shape
    bs, ls, _ = x_sample.shape
    depth, _, p, kv_lora = cache_ckv.shape
    rope = cache_kpe.shape[-1]
    q_lora = g_q.shape[-1]
    nope = w_uk.shape[-1] // N_HEADS
    v_dim = w_uv.shape[-1] // N_HEADS
    assert nope == LANE and v_dim == LANE and rope * 2 == LANE and d == N_HEADS * v_dim
    assert conv_w.shape[1] == CONV_WIDTH and s % CHUNK == 0 and ls >= CONV_WIDTH - 1
    q_scale = float(nope + rope) ** -0.5 * float(np.log2(np.e))
    dims = (d, q_lora, kv_lora, rope, q_scale)

    off_pe_end = 3 * d + q_lora + kv_lora + rope
    w_all = jnp.concatenate(
        [w_in[:, :, :off_pe_end], jnp.zeros((depth, d, LANE - rope), w_in.dtype),
         w_in[:, :, off_pe_end:]], axis=2).astype(BF16)
    wuq = jnp.pad(w_uq.reshape(depth, q_lora, N_HEADS, nope + rope),
                  ((0, 0), (0, 0), (0, 0), (0, HEAD_PAD - nope - rope))
                  ).reshape(depth, q_lora, N_HEADS * HEAD_PAD).astype(BF16)
    wuk, wuv = w_uk.astype(BF16), w_uv.astype(BF16)
    wo_b, w1_b, w2_b = w_o.astype(BF16), w1.astype(BF16), w2.astype(BF16)
    g3 = lambda g: g[:, None, :]
    proj_w = (w_all, wuq, wuk, wuv, conv_w, g3(g_pre_mix), g3(g_q), g3(g_kv))
    post_w = (wo_b, w1_b, w2_b, g3(g_post_mix), g3(g_pre_ffn), g3(g_post_ffn))

    tm_p = _tile(s, 512)
    tq = _tile(s, 512)
    segs_per_tile = max(1, min(bs, 512 // ls))
    assert bs % segs_per_tile == 0
    tm_s = segs_per_tile * ls
    tabs_p = _rope_tables(jnp.arange(s, dtype=jnp.int32), rope)
    tabs_s = tuple(jnp.tile(t, (segs_per_tile, 1))
                   for t in _rope_tables(p + jnp.arange(ls, dtype=jnp.int32), rope))
    pos_new = p + np.arange(ls)
    new_all_visible = bool(np.all((pos_new[None, :] // CHUNK) <= (pos_new[:, None] // CHUNK)))
    new_mask = None if new_all_visible else p

    xp = x_prompt.reshape(b * s, d)
    xs = x_sample.reshape(bs * ls, d)
    zero_buf = jnp.zeros((b, CONV_WIDTH - 1, d), F32)
    ckv_p, kpe_p, conv_p, ckv_s, kpe_s, conv_s = [], [], [], [], [], []
    for l in range(depth):
        ya, sgb, q, ckv, kpe, cbuf, k, v = _proj_call(
            xp, zero_buf, tabs_p, proj_w, l, n_outer=b, n_inner=s // tm_p, tm=tm_p, seg=tm_p,
            dims=dims, expand_kv=True)
        yb = _attn_prompt_call(q.reshape(b, s, -1), k.reshape(b, s, -1), v.reshape(b, s, -1),
                               b=b, s=s, tq=tq)
        xp = _post_call(xp, ya, sgb, yb.reshape(b * s, d), post_w, l, tm=tm_p)
        ckv_p.append(ckv.reshape(b, s, kv_lora))
        kpe_p.append(kpe.reshape(b, s, rope))
        conv_p.append(cbuf)

        ya, sgb, q, ckv, kpe, cbuf = _proj_call(
            xs, state_conv[l], tabs_s, proj_w, l, n_outer=bs // segs_per_tile, n_inner=1,
            tm=tm_s, seg=ls, dims=dims, expand_kv=False)
        ckv3, kpe3 = ckv.reshape(bs, ls, kv_lora), kpe.reshape(bs, ls, rope)
        yb = _attn_sample_call(q.reshape(bs, ls, -1), cache_ckv, cache_kpe, ckv3, kpe3,
                               wuk, wuv, l, b=bs, ls=ls, new_mask=new_mask)
        xs = _post_call(xs, ya, sgb, yb.reshape(bs * ls, d), post_w, l, tm=tm_s)
        ckv_s.append(ckv3)
        kpe_s.append(kpe3)
        conv_s.append(cbuf)

    return (xp.reshape(b, s, d), xs.reshape(bs, ls, d),
            jnp.stack(ckv_p), jnp.stack(kpe_p), jnp.stack(conv_p),
            jnp.stack(ckv_s), jnp.stack(kpe_s), jnp.stack(conv_s))
```

```python
import functools

import numpy as np
import jax
import jax.numpy as jnp
from jax import lax
from jax.experimental import pallas as pl
from jax.experimental.pallas import tpu as pltpu

CHUNK = 64
CONV_WIDTH = 3
N_HEADS = 8
ROPE_THETA = 10000.0
EPS = 1e-6
NEG_INF = -1e30

LANE = 128
SUBLANE = 8
HEAD_PAD = 2 * LANE
VMEM_LIMIT_BYTES = 56 * 1024 * 1024
POST_ROW_GROUPS = 2

F32 = jnp.float32
BF16 = jnp.bfloat16


def _rms(x, g):
    ms = jnp.mean(x * x, axis=-1, keepdims=True)
    return x * lax.rsqrt(ms + EPS) * g


def _rope(pe, cos_t, sin_a, sin_b):
    return (pe * cos_t + pltpu.roll(pe, LANE // 4, 1) * sin_a
            + pltpu.roll(pe, LANE - LANE // 4, 1) * sin_b)


def _dot(a, b):
    return jnp.dot(a, b, preferred_element_type=F32)


def _dot_nt(a, b):
    return lax.dot_general(a, b, (((1,), (1,)), ((), ())), preferred_element_type=F32)


def _proj_kernel(x_ref, init_ref, cos_ref, sina_ref, sinb_ref,
                 wa_ref, wb_ref, wg_ref, wuq_ref, wuk_ref, wuv_ref, convw_ref,
                 gpre_ref, gq_ref, gkv_ref,
                 ya_ref, sgb_ref, q_ref, ckv_ref, kpe_ref, cbuf_ref, *rest,
                 d, seg, nseg, q_lora, kv_lora, rope, sm_scale, expand_kv):
    if expand_kv:
        k_ref, v_ref, carry_ref = rest
    else:
        (carry_ref,) = rest
    inner = pl.program_id(1)

    h = _rms(x_ref[...], gpre_ref[...]).astype(BF16)
    cos_t, sin_a, sin_b = cos_ref[...], sina_ref[...], sinb_ref[...]

    off_q = 2 * d
    off_kv = q_lora
    off_pe = off_kv + kv_lora
    lat = _dot(h, wa_ref[:, off_q:])
    zc = _dot(h, wa_ref[:, :d])

    cqn = _rms(lat[:, :off_kv], gq_ref[...]).astype(BF16)
    qn = _dot(cqn, wuq_ref[:, :N_HEADS * LANE]) * sm_scale
    qp = _dot(cqn, wuq_ref[:, N_HEADS * LANE:]) * sm_scale
    for hh in range(N_HEADS):
        pe = qp[:, (hh // 2) * LANE:(hh // 2 + 1) * LANE]
        if hh % 2:
            pe = pltpu.roll(pe, LANE // 2, 1)
        q_ref[:, hh * HEAD_PAD:hh * HEAD_PAD + LANE] = qn[:, hh * LANE:(hh + 1) * LANE].astype(BF16)
        q_ref[:, hh * HEAD_PAD + LANE:(hh + 1) * HEAD_PAD] = _rope(
            pe, cos_t, sin_a, sin_b).astype(BF16)

    ckv = _rms(lat[:, off_kv:off_pe], gkv_ref[...])
    ckv_ref[...] = ckv
    kpe = _rope(lat[:, off_pe:], cos_t, sin_a, sin_b)
    kpe_ref[...] = kpe[:, :rope]
    if expand_kv:
        ckvn = ckv.astype(BF16)
        kpe_b = kpe.astype(BF16)
        for hp in range(N_HEADS // 2):
            kn = _dot(ckvn, wuk_ref[:, hp * HEAD_PAD:(hp + 1) * HEAD_PAD]).astype(BF16)
            base = 2 * hp * HEAD_PAD
            k_ref[:, base:base + LANE] = kn[:, :LANE]
            k_ref[:, base + LANE:base + HEAD_PAD] = kpe_b
            k_ref[:, base + HEAD_PAD:base + HEAD_PAD + LANE] = kn[:, LANE:]
            k_ref[:, base + HEAD_PAD + LANE:base + 2 * HEAD_PAD] = kpe_b
        v_ref[...] = _dot(ckvn, wuv_ref[...]).astype(BF16)

    u = zc * _dot(h, wa_ref[:, d:off_q])
    w0 = convw_ref[0:1, :]
    w1 = convw_ref[1:2, :]
    w2 = convw_ref[2:3, :]
    rows = lax.broadcasted_iota(jnp.int32, (SUBLANE, 1), 0)

    @pl.when(inner == 0)
    def _():
        carry_ref[...] = init_ref[...]

    yc_parts = []
    for s in range(nseg):
        us = u[s * seg:(s + 1) * seg]
        init = carry_ref[s]
        i0, i1 = init[0:1], init[1:2]
        r1, r2 = pltpu.roll(us, 1, 0), pltpu.roll(us, 2, 0)
        h1 = jnp.where(rows == 0, i1, r1[:SUBLANE])
        h2 = jnp.where(rows == 0, i0, jnp.where(rows == 1, i1, r2[:SUBLANE]))
        u1 = jnp.concatenate([h1, r1[SUBLANE:]], axis=0)
        u2 = jnp.concatenate([h2, r2[SUBLANE:]], axis=0)
        yc_parts.append(w0 * u2 + w1 * u1 + w2 * us)
        tail = us[seg - 2:seg]
        carry_ref[s] = tail
        cbuf_ref[s] = tail
    yc = yc_parts[0] if nseg == 1 else jnp.concatenate(yc_parts, axis=0)
    ya = jax.nn.sigmoid(_dot(h, wg_ref[:, :d])) * _dot(h, wb_ref[...]) * yc
    ya_ref[...] = ya.astype(BF16)
    sgb_ref[...] = jax.nn.sigmoid(_dot(h, wg_ref[:, d:])).astype(BF16)


def _proj_call(x2d, conv_init, tabs, wts, l, *, n_outer, n_inner, tm, seg, dims, expand_kv):
    d, q_lora, kv_lora, rope, sm_scale = dims
    nseg = tm // seg
    rows = x2d.shape[0]
    row_map = lambda o, i: (o * n_inner + i, 0)
    tab_map = lambda o, i: (i, 0)
    seg_map = lambda o, i: (o, 0, 0)
    lay_map = lambda o, i: (l, 0, 0)

    def wspec(a):
        return pl.BlockSpec((None,) + a.shape[1:], lay_map, pipeline_mode=pl.Buffered(1))

    def rspec(n):
        return pl.BlockSpec((tm, n), row_map)

    kern = functools.partial(_proj_kernel, d=d, seg=seg, nseg=nseg, q_lora=q_lora,
                             kv_lora=kv_lora, rope=rope, sm_scale=sm_scale,
                             expand_kv=expand_kv)
    out_shape = [
        jax.ShapeDtypeStruct((rows, d), BF16),
        jax.ShapeDtypeStruct((rows, d), BF16),
        jax.ShapeDtypeStruct((rows, N_HEADS * HEAD_PAD), BF16),
        jax.ShapeDtypeStruct((rows, kv_lora), F32),
        jax.ShapeDtypeStruct((rows, rope), F32),
        jax.ShapeDtypeStruct(conv_init.shape, F32),
    ]
    out_specs = [rspec(d), rspec(d), rspec(N_HEADS * HEAD_PAD), rspec(kv_lora), rspec(rope),
                 pl.BlockSpec((nseg, CONV_WIDTH - 1, d), seg_map)]
    if expand_kv:
        out_shape += [jax.ShapeDtypeStruct((rows, N_HEADS * HEAD_PAD), BF16),
                      jax.ShapeDtypeStruct((rows, N_HEADS * LANE), BF16)]
        out_specs += [rspec(N_HEADS * HEAD_PAD), rspec(N_HEADS * LANE)]
    return pl.pallas_call(
        kern,
        out_shape=out_shape,
        grid=(n_outer, n_inner),
        in_specs=[rspec(d), pl.BlockSpec((nseg, CONV_WIDTH - 1, d), seg_map),
                  pl.BlockSpec((tm, LANE), tab_map), pl.BlockSpec((tm, LANE), tab_map),
                  pl.BlockSpec((tm, LANE), tab_map),
                  *[wspec(w) for w in wts]],
        out_specs=out_specs,
        scratch_shapes=[pltpu.VMEM((nseg, CONV_WIDTH - 1, d), F32)],
        compiler_params=pltpu.CompilerParams(
            dimension_semantics=("arbitrary", "arbitrary"),
            vmem_limit_bytes=VMEM_LIMIT_BYTES),
        name="proj",
    )(x2d, conv_init, *tabs, *wts)


def _attn_prompt_kernel(q_ref, k_ref, v_ref, o_ref, m_ref, acc_ref, *, tq, nq):
    i = pl.program_id(1)
    m_ref[...] = jnp.full(m_ref.shape, NEG_INF, F32)
    acc_ref[...] = jnp.zeros(acc_ref.shape, F32)
    ncol = tq // LANE
    r_chunk = lax.broadcasted_iota(jnp.int32, (tq, LANE), 0) // CHUNK
    ones = jnp.ones((tq, LANE), BF16)

    def block(j, masked):
        koff = j * tq
        for hh in range(N_HEADS):
            q = q_ref[:, hh * HEAD_PAD:(hh + 1) * HEAD_PAD]
            k = k_ref[koff:koff + tq, hh * HEAD_PAD:(hh + 1) * HEAD_PAD]
            v = v_ref[koff:koff + tq, hh * LANE:(hh + 1) * LANE]
            s = _dot_nt(q, k)
            cols = []
            for c in range(ncol):
                sc = s[:, c * LANE:(c + 1) * LANE]
                if masked:
                    c_chunk = (lax.broadcasted_iota(jnp.int32, (tq, LANE), 1) + c * LANE) // CHUNK
                    sc = jnp.where(c_chunk <= r_chunk, sc, NEG_INF)
                cols.append(sc)
            m_prev = m_ref[hh]
            m_cur = functools.reduce(jnp.maximum, cols)
            m_next = jnp.maximum(m_prev, jnp.max(m_cur, axis=1, keepdims=True))
            alpha = jnp.exp2(m_prev - m_next)
            p = jnp.concatenate([jnp.exp2(sc - m_next) for sc in cols], axis=1).astype(BF16)
            pv = _dot(p, jnp.concatenate([v, ones], axis=1))
            acc_ref[hh] = jnp.concatenate([alpha, alpha], axis=1) * acc_ref[hh] + pv
            m_ref[hh] = m_next

    for qi in range(nq):
        @pl.when(i == qi)
        def _(qi=qi):
            for j in range(qi):
                block(j, False)
            block(qi, True)

    for hh in range(N_HEADS):
        acc = acc_ref[hh]
        o_ref[:, hh * LANE:(hh + 1) * LANE] = (acc[:, :LANE] / acc[:, LANE:]).astype(BF16)


def _attn_prompt_call(q, k, v, *, b, s, tq):
    nq = s // tq
    return pl.pallas_call(
        functools.partial(_attn_prompt_kernel, tq=tq, nq=nq),
        out_shape=jax.ShapeDtypeStruct((b, s, N_HEADS * LANE), BF16),
        grid=(b, nq),
        in_specs=[pl.BlockSpec((None, tq, N_HEADS * HEAD_PAD), lambda bb, i: (bb, i, 0)),
                  pl.BlockSpec((None, s, N_HEADS * HEAD_PAD), lambda bb, i: (bb, 0, 0)),
                  pl.BlockSpec((None, s, N_HEADS * LANE), lambda bb, i: (bb, 0, 0))],
        out_specs=pl.BlockSpec((None, tq, N_HEADS * LANE), lambda bb, i: (bb, i, 0)),
        scratch_shapes=[pltpu.VMEM((N_HEADS, tq, LANE), F32),
                        pltpu.VMEM((N_HEADS, tq, 2 * LANE), F32)],
        compiler_params=pltpu.CompilerParams(
            dimension_semantics=("arbitrary", "arbitrary"),
            vmem_limit_bytes=VMEM_LIMIT_BYTES),
        name="attn_prompt",
    )(q, k, v)


def _attn_sample_kernel(q_ref, pckv_ref, pkpe_t_ref, nckv_ref, nkpe_ref, wuk_ref, wuv_ref,
                        o_ref, *, ls, rope, new_mask):
    qa, qp = [], []
    for hh in range(N_HEADS):
        qn = q_ref[:, hh * HEAD_PAD:hh * HEAD_PAD + LANE]
        qa.append(_dot_nt(qn, wuk_ref[:, hh * LANE:(hh + 1) * LANE]).astype(BF16))
        qp.append(q_ref[:, hh * HEAD_PAD + LANE:hh * HEAD_PAD + LANE + rope])
    qa = jnp.concatenate(qa, axis=0)
    qp = jnp.concatenate(qp, axis=0)
    pckv = pckv_ref[...].astype(BF16)
    nckv = nckv_ref[...].astype(BF16)
    s_past = _dot_nt(qa, pckv) + _dot(qp, pkpe_t_ref[...].astype(BF16))
    s_new = _dot_nt(qa, nckv) + _dot_nt(qp, nkpe_ref[...].astype(BF16))
    if new_mask is not None:
        r = lax.broadcasted_iota(jnp.int32, s_new.shape, 0) % ls
        c = lax.broadcasted_iota(jnp.int32, s_new.shape, 1)
        p0 = new_mask
        s_new = jnp.where((p0 + c) // CHUNK <= (p0 + r) // CHUNK, s_new, NEG_INF)
    m = jnp.maximum(jnp.max(s_past, axis=1, keepdims=True),
                    jnp.max(s_new, axis=1, keepdims=True))
    p_past = jnp.exp2(s_past - m)
    p_new = jnp.exp2(s_new - m)
    denom = jnp.sum(p_past, axis=1, keepdims=True) + jnp.sum(p_new, axis=1, keepdims=True)
    o_lat = (_dot(p_past.astype(BF16), pckv) + _dot(p_new.astype(BF16), nckv)) / denom
    o_lat = o_lat.astype(BF16)
    for hh in range(N_HEADS):
        o_ref[:, hh * LANE:(hh + 1) * LANE] = _dot(
            o_lat[hh * ls:(hh + 1) * ls], wuv_ref[:, hh * LANE:(hh + 1) * LANE]).astype(BF16)


def _attn_sample_call(q, cache_ckv, cache_kpe_t, nckv, nkpe, wuk, wuv, l, *, b, ls, new_mask):
    p, kv_lora = cache_ckv.shape[2], cache_ckv.shape[3]
    rope = cache_kpe_t.shape[2]
    bmap = lambda bb: (bb, 0, 0)
    return pl.pallas_call(
        functools.partial(_attn_sample_kernel, ls=ls, rope=rope, new_mask=new_mask),
        out_shape=jax.ShapeDtypeStruct((b, ls, N_HEADS * LANE), BF16),
        grid=(b,),
        in_specs=[pl.BlockSpec((None, ls, N_HEADS * HEAD_PAD), bmap),
                  pl.BlockSpec((None, None, p, kv_lora), lambda bb: (l, bb, 0, 0)),
                  pl.BlockSpec((None, None, rope, p), lambda bb: (l, bb, 0, 0)),
                  pl.BlockSpec((None, ls, kv_lora), bmap),
                  pl.BlockSpec((None, ls, rope), bmap),
                  pl.BlockSpec((None,) + wuk.shape[1:], lambda bb: (l, 0, 0)),
                  pl.BlockSpec((None,) + wuv.shape[1:], lambda bb: (l, 0, 0))],
        out_specs=pl.BlockSpec((None, ls, N_HEADS * LANE), bmap),
        compiler_params=pltpu.CompilerParams(
            dimension_semantics=("arbitrary",),
            vmem_limit_bytes=VMEM_LIMIT_BYTES),
        name="attn_sample",
    )(q, cache_ckv, cache_kpe_t, nckv, nkpe, wuk, wuv)


def _post_kernel(x_ref, ya_ref, sgb_ref, yb_ref, wo_ref, w1_ref, w2_ref,
                 gpm_ref, gpf_ref, gqf_ref, o_ref, *, d_ff, ff_chunk, n_split):
    tm = x_ref.shape[0]
    groups = [pl.ds(i * (tm // n_split), tm // n_split) for i in range(n_split)]
    n_chunks = d_ff // ff_chunk
    up = lambda h2, c: _dot(h2, w1_ref[:, c * ff_chunk:(c + 1) * ff_chunk])

    outs = []
    for r in groups:
        merged = ya_ref[r, :].astype(F32) + sgb_ref[r, :].astype(F32) * yb_ref[r, :].astype(F32)
        outs.append(_dot(merged.astype(BF16), wo_ref[...]))
    x1s = [x_ref[r, :] + _rms(o, gpm_ref[...]) for r, o in zip(groups, outs)]
    h2s = [_rms(x1, gpf_ref[...]).astype(BF16) for x1 in x1s]
    a_next = [up(h2, 0) for h2 in h2s]
    accs = [jnp.zeros(x1.shape, F32) for x1 in x1s]
    for c in range(n_chunks):
        for i, h2 in enumerate(h2s):
            a = jnp.square(jnp.maximum(a_next[i], 0.0)).astype(BF16)
            if c + 1 < n_chunks:
                a_next[i] = up(h2, c + 1)
            accs[i] = accs[i] + _dot(a, w2_ref[c * ff_chunk:(c + 1) * ff_chunk, :])
    for r, x1, acc in zip(groups, x1s, accs):
        o_ref[r, :] = x1 + _rms(acc, gqf_ref[...])


def _post_call(x2d, ya, sgb, yb, wts, l, *, tm):
    rows, d = x2d.shape
    wo, w1, w2, gpm, gpf, gqf = wts
    d_ff = w1.shape[-1]
    row_map = lambda i: (i, 0)
    lay_map = lambda i: (l, 0, 0)

    def wspec(a):
        return pl.BlockSpec((None,) + a.shape[1:], lay_map, pipeline_mode=pl.Buffered(1))

    rspec = pl.BlockSpec((tm, d), row_map)
    return pl.pallas_call(
        functools.partial(_post_kernel, d_ff=d_ff, ff_chunk=min(d_ff, 1024),
                          n_split=POST_ROW_GROUPS if tm % (POST_ROW_GROUPS * 2 * SUBLANE) == 0 else 1),
        out_shape=jax.ShapeDtypeStruct((rows, d), F32),
        grid=(rows // tm,),
        in_specs=[rspec, rspec, rspec, rspec, wspec(wo), wspec(w1), wspec(w2),
                  wspec(gpm), wspec(gpf), wspec(gqf)],
        out_specs=rspec,
        compiler_params=pltpu.CompilerParams(
            dimension_semantics=("arbitrary",),
            vmem_limit_bytes=VMEM_LIMIT_BYTES),
        name="post",
    )(x2d, ya, sgb, yb, wo, w1, w2, gpm, gpf, gqf)


def _rope_tables(pos, rope):
    inv = ROPE_THETA ** (-jnp.arange(0, rope, 2, dtype=F32) / rope)
    ang = pos.astype(F32)[:, None] * inv[None, :]
    cos, sin = jnp.cos(ang), jnp.sin(ang)
    z = jnp.zeros_like(cos)
    pad = jnp.zeros((pos.shape[0], LANE - rope), F32)
    cos_t = jnp.concatenate([cos, cos, pad], axis=1)
    sin_a = jnp.concatenate([z, sin, pad], axis=1)
    sin_b = jnp.concatenate([-sin, z, pad], axis=1)
    return cos_t, sin_a, sin_b


def _prep_weights(w_in, conv_w, w_uq, w_uk, w_uv, w_o, w1, w2, g_pre_mix, g_post_mix, g_pre_ffn,
                  g_post_ffn, g_q, g_kv, *, nope, rope):
    depth, d, _ = w_in.shape
    q_lora = w_uq.shape[1]
    off_ga = w_in.shape[-1] - 2 * d
    w_a = jnp.pad(w_in[:, :, d:off_ga],
                  ((0, 0), (0, 0), (0, LANE - rope))).astype(BF16)
    w_b = w_in[:, :, :d].astype(BF16)
    w_g = w_in[:, :, off_ga:].astype(BF16)
    uq = w_uq.reshape(depth, q_lora, N_HEADS, nope + rope)
    wuq = jnp.concatenate(
        [uq[..., :nope].reshape(depth, q_lora, N_HEADS * nope),
         uq[..., nope:].reshape(depth, q_lora, N_HEADS * rope)], axis=2).astype(BF16)
    wuk, wuv = w_uk.astype(BF16), w_uv.astype(BF16)
    g3 = lambda g: g[:, None, :]
    proj_w = (w_a, w_b, w_g, wuq, wuk, wuv, conv_w, g3(g_pre_mix), g3(g_q), g3(g_kv))
    post_w = (w_o.astype(BF16), w1.astype(BF16), w2.astype(BF16),
              g3(g_post_mix), g3(g_pre_ffn), g3(g_post_ffn))
    return proj_w, post_w, wuk, wuv


def _tile(n, pref):
    t = min(n, pref)
    assert n % t == 0, (n, t)
    return t


def kernel(x_prompt, x_sample, cache_ckv, cache_kpe, state_conv, w_in, conv_w, w_uq, w_uk, w_uv,
           w_o, w1, w2, g_pre_mix, g_post_mix, g_pre_ffn, g_post_ffn, g_q, g_kv):
    b, s, d = x_prompt.shape
    bs, ls, _ = x_sample.shape
    depth, _, p, kv_lora = cache_ckv.shape
    rope = cache_kpe.shape[-1]
    q_lora = g_q.shape[-1]
    nope = w_uk.shape[-1] // N_HEADS
    v_dim = w_uv.shape[-1] // N_HEADS
    assert nope == LANE and v_dim == LANE and rope * 2 == LANE and d == N_HEADS * v_dim
    assert conv_w.shape[1] == CONV_WIDTH and s % CHUNK == 0 and ls >= CONV_WIDTH - 1
    q_scale = float(nope + rope) ** -0.5 * float(np.log2(np.e))
    dims = (d, q_lora, kv_lora, rope, q_scale)

    assert q_lora % LANE == 0 and kv_lora % LANE == 0
    proj_w, post_w, wuk, wuv = _prep_weights(
        w_in, conv_w, w_uq, w_uk, w_uv, w_o, w1, w2, g_pre_mix, g_post_mix, g_pre_ffn,
        g_post_ffn, g_q, g_kv, nope=nope, rope=rope)
    cache_kpe_t = jnp.swapaxes(cache_kpe, 2, 3)

    tm_p = _tile(s, 512)
    tq = _tile(s, 512)
    segs_per_tile = max(1, min(bs, 512 // ls))
    assert bs % segs_per_tile == 0
    tm_s = segs_per_tile * ls
    tabs_p = _rope_tables(jnp.arange(s, dtype=jnp.int32), rope)
    tabs_s = tuple(jnp.tile(t, (segs_per_tile, 1))
                   for t in _rope_tables(p + jnp.arange(ls, dtype=jnp.int32), rope))
    pos_new = p + np.arange(ls)
    new_all_visible = bool(np.all((pos_new[None, :] // CHUNK) <= (pos_new[:, None] // CHUNK)))
    new_mask = None if new_all_visible else p

    xp = x_prompt.reshape(b * s, d)
    xs = x_sample.reshape(bs * ls, d)
    zero_buf = jnp.zeros((b, CONV_WIDTH - 1, d), F32)
    ckv_p, kpe_p, conv_p, ckv_s, kpe_s, conv_s = [], [], [], [], [], []
    for l in range(depth):
        ya, sgb, q, ckv, kpe, cbuf, k, v = _proj_call(
            xp, zero_buf, tabs_p, proj_w, l, n_outer=b, n_inner=s // tm_p, tm=tm_p, seg=tm_p,
            dims=dims, expand_kv=True)
        yb = _attn_prompt_call(q.reshape(b, s, -1), k.reshape(b, s, -1), v.reshape(b, s, -1),
                               b=b, s=s, tq=tq)
        xp = _post_call(xp, ya, sgb, yb.reshape(b * s, d), post_w, l, tm=tm_p)
        ckv_p.append(ckv.reshape(b, s, kv_lora))
        kpe_p.append(kpe.reshape(b, s, rope))
        conv_p.append(cbuf)

        ya, sgb, q, ckv, kpe, cbuf = _proj_call(
            xs, state_conv[l], tabs_s, proj_w, l, n_outer=bs // segs_per_tile, n_inner=1,
            tm=tm_s, seg=ls, dims=dims, expand_kv=False)
        ckv3, kpe3 = ckv.reshape(bs, ls, kv_lora), kpe.reshape(bs, ls, rope)
        yb = _attn_sample_call(q.reshape(bs, ls, -1), cache_ckv, cache_kpe_t, ckv3, kpe3,
                               wuk, wuv, l, b=bs, ls=ls, new_mask=new_mask)
        xs = _post_call(xs, ya, sgb, yb.reshape(bs * ls, d), post_w, l, tm=tm_s)
        ckv_s.append(ckv3)
        kpe_s.append(kpe3)
        conv_s.append(cbuf)

    return (xp.reshape(b, s, d), xs.reshape(bs, ls, d),
            jnp.stack(ckv_p), jnp.stack(kpe_p), jnp.stack(conv_p),
            jnp.stack(ckv_s), jnp.stack(kpe_s), jnp.stack(conv_s))
```

```python
import functools

import numpy as np
import jax
import jax.numpy as jnp
from jax import lax
from jax.experimental import pallas as pl
from jax.experimental.pallas import tpu as pltpu

CHUNK = 64
CONV_WIDTH = 3
N_HEADS = 8
ROPE_THETA = 10000.0
EPS = 1e-6
NEG_INF = -1e30

LANE = 128
SUBLANE = 8
HEAD_PAD = 2 * LANE
VMEM_LIMIT_BYTES = 56 * 1024 * 1024
POST_ROW_GROUPS = 2
PROJ_ROW_GROUPS = 2

F32 = jnp.float32
BF16 = jnp.bfloat16


def _rms(x, g):
    ms = jnp.mean(x * x, axis=-1, keepdims=True)
    return x * lax.rsqrt(ms + EPS) * g


def _rope(pe, cos_t, sin_a, sin_b):
    return (pe * cos_t + pltpu.roll(pe, LANE // 4, 1) * sin_a
            + pltpu.roll(pe, LANE - LANE // 4, 1) * sin_b)


def _dot(a, b):
    return jnp.dot(a, b, preferred_element_type=F32)


def _dot_nt(a, b):
    return lax.dot_general(a, b, (((1,), (1,)), ((), ())), preferred_element_type=F32)


def _proj_kernel(x_ref, init_ref, cos_ref, sina_ref, sinb_ref,
                 wa_ref, wb_ref, wg_ref, wuq_ref, wuk_ref, wuv_ref, convw_ref,
                 gpre_ref, gq_ref, gkv_ref, *rest,
                 d, seg, n_groups, q_lora, kv_lora, rope, sm_scale, expand_kv, n_prev):
    ya_ref, sgb_ref, q_ref, ckv_ref, kpe_ref, cbuf_ref, *rest = rest[n_prev:]
    if expand_kv:
        k_ref, v_ref, carry_ref = rest
    else:
        (carry_ref,) = rest
    inner = pl.program_id(1)
    gm = x_ref.shape[0] // n_groups
    groups = [pl.ds(g * gm, gm) for g in range(n_groups)]
    off_q = 2 * d
    off_kv = q_lora
    off_pe = off_kv + kv_lora

    @pl.when(inner == 0)
    def _():
        carry_ref[...] = init_ref[...]

    hs = [_rms(x_ref[r, :], gpre_ref[...]).astype(BF16) for r in groups]
    tabs = [(cos_ref[r, :], sina_ref[r, :], sinb_ref[r, :]) for r in groups]
    lats = [_dot(h, wa_ref[:, off_q:]) for h in hs]
    zcs = [_dot(h, wa_ref[:, :d]) for h in hs]

    for r, lat, tab in zip(groups, lats, tabs):
        cqn = _rms(lat[:, :off_kv], gq_ref[...]).astype(BF16)
        qn = _dot(cqn, wuq_ref[:, :N_HEADS * LANE]) * sm_scale
        qp = _dot(cqn, wuq_ref[:, N_HEADS * LANE:]) * sm_scale
        for hh in range(N_HEADS):
            pe = qp[:, (hh // 2) * LANE:(hh // 2 + 1) * LANE]
            if hh % 2:
                pe = pltpu.roll(pe, LANE // 2, 1)
            q_ref[r, hh * HEAD_PAD:hh * HEAD_PAD + LANE] = qn[:, hh * LANE:(hh + 1) * LANE].astype(BF16)
            q_ref[r, hh * HEAD_PAD + LANE:(hh + 1) * HEAD_PAD] = _rope(pe, *tab).astype(BF16)

    for r, lat, tab in zip(groups, lats, tabs):
        ckv = _rms(lat[:, off_kv:off_pe], gkv_ref[...])
        ckv_ref[r, :] = ckv
        kpe = _rope(lat[:, off_pe:], *tab)
        kpe_ref[r, :] = kpe[:, :rope]
        if expand_kv:
            ckvn = ckv.astype(BF16)
            kpe_b = kpe.astype(BF16)
            for hp in range(N_HEADS // 2):
                kn = _dot(ckvn, wuk_ref[:, hp * HEAD_PAD:(hp + 1) * HEAD_PAD]).astype(BF16)
                base = 2 * hp * HEAD_PAD
                k_ref[r, base:base + LANE] = kn[:, :LANE]
                k_ref[r, base + LANE:base + HEAD_PAD] = kpe_b
                k_ref[r, base + HEAD_PAD:base + HEAD_PAD + LANE] = kn[:, LANE:]
                k_ref[r, base + HEAD_PAD + LANE:base + 2 * HEAD_PAD] = kpe_b
            v_ref[r, :] = _dot(ckvn, wuv_ref[...]).astype(BF16)

    us = [zc * _dot(h, wa_ref[:, d:off_q]) for zc, h in zip(zcs, hs)]
    w0 = convw_ref[0:1, :]
    w1 = convw_ref[1:2, :]
    w2 = convw_ref[2:3, :]
    rows = lax.broadcasted_iota(jnp.int32, (SUBLANE, 1), 0)
    pm = min(seg, gm)
    ycs, tail = [], None
    for g, u in enumerate(us):
        parts = []
        for p in range(gm // pm):
            row0 = g * gm + p * pm
            sidx = row0 // seg
            up = u[p * pm:(p + 1) * pm]
            init = carry_ref[sidx] if row0 % seg == 0 else tail
            i0, i1 = init[0:1], init[1:2]
            r1, r2 = pltpu.roll(up, 1, 0), pltpu.roll(up, 2, 0)
            h1 = jnp.where(rows == 0, i1, r1[:SUBLANE])
            h2 = jnp.where(rows == 0, i0, jnp.where(rows == 1, i1, r2[:SUBLANE]))
            u1 = jnp.concatenate([h1, r1[SUBLANE:]], axis=0)
            u2 = jnp.concatenate([h2, r2[SUBLANE:]], axis=0)
            parts.append(w0 * u2 + w1 * u1 + w2 * up)
            tail = up[pm - 2:pm]
            if (row0 + pm) % seg == 0:
                carry_ref[sidx] = tail
                cbuf_ref[sidx] = tail
        ycs.append(parts[0] if len(parts) == 1 else jnp.concatenate(parts, axis=0))

    for r, h, yc in zip(groups, hs, ycs):
        ya = jax.nn.sigmoid(_dot(h, wg_ref[:, :d])) * _dot(h, wb_ref[...]) * yc
        ya_ref[r, :] = ya.astype(BF16)
    for r, h in zip(groups, hs):
        sgb_ref[r, :] = jax.nn.sigmoid(_dot(h, wg_ref[:, d:])).astype(BF16)


def _proj_call(x2d, conv_init, tabs, wts, l, prev_stacks, *, n_outer, n_inner, tm, seg, dims,
               expand_kv):
    d, q_lora, kv_lora, rope, sm_scale = dims
    depth = wts[0].shape[0]
    nseg = tm // seg
    rows = x2d.shape[0]
    row_map = lambda o, i: (o * n_inner + i, 0)
    tab_map = lambda o, i: (i, 0)
    seg_map = lambda o, i: (o, 0, 0)
    lay_map = lambda o, i: (l, 0, 0)
    lrow_map = lambda o, i: (l, o * n_inner + i, 0)

    def wspec(a):
        return pl.BlockSpec((None,) + a.shape[1:], lay_map, pipeline_mode=pl.Buffered(1))

    def rspec(n):
        return pl.BlockSpec((tm, n), row_map)

    gm = tm // PROJ_ROW_GROUPS
    n_groups = PROJ_ROW_GROUPS if gm % (2 * SUBLANE) == 0 and (gm % seg == 0 or seg % gm == 0) else 1
    kern = functools.partial(_proj_kernel, d=d, seg=seg, n_groups=n_groups, q_lora=q_lora,
                             kv_lora=kv_lora, rope=rope, sm_scale=sm_scale,
                             expand_kv=expand_kv, n_prev=len(prev_stacks))
    out_shape = [
        jax.ShapeDtypeStruct((rows, d), BF16),
        jax.ShapeDtypeStruct((rows, d), BF16),
        jax.ShapeDtypeStruct((rows, N_HEADS * HEAD_PAD), BF16),
        jax.ShapeDtypeStruct((depth, rows, kv_lora), F32),
        jax.ShapeDtypeStruct((depth, rows, rope), F32),
        jax.ShapeDtypeStruct((depth,) + conv_init.shape, F32),
    ]
    out_specs = [rspec(d), rspec(d), rspec(N_HEADS * HEAD_PAD),
                 pl.BlockSpec((None, tm, kv_lora), lrow_map),
                 pl.BlockSpec((None, tm, rope), lrow_map),
                 pl.BlockSpec((None, nseg, CONV_WIDTH - 1, d), lambda o, i: (l, o, 0, 0))]
    if expand_kv:
        out_shape += [jax.ShapeDtypeStruct((rows, N_HEADS * HEAD_PAD), BF16),
                      jax.ShapeDtypeStruct((rows, N_HEADS * LANE), BF16)]
        out_specs += [rspec(N_HEADS * HEAD_PAD), rspec(N_HEADS * LANE)]
    in_specs = [rspec(d), pl.BlockSpec((nseg, CONV_WIDTH - 1, d), seg_map),
                pl.BlockSpec((tm, LANE), tab_map), pl.BlockSpec((tm, LANE), tab_map),
                pl.BlockSpec((tm, LANE), tab_map),
                *[wspec(w) for w in wts]]
    n_in = len(in_specs)
    in_specs += [pl.BlockSpec(memory_space=pl.ANY)] * len(prev_stacks)
    stack_out0 = 3
    return pl.pallas_call(
        kern,
        out_shape=out_shape,
        grid=(n_outer, n_inner),
        in_specs=in_specs,
        out_specs=out_specs,
        scratch_shapes=[pltpu.VMEM((nseg, CONV_WIDTH - 1, d), F32)],
        input_output_aliases={n_in + j: stack_out0 + j for j in range(len(prev_stacks))},
        compiler_params=pltpu.CompilerParams(
            dimension_semantics=("arbitrary", "arbitrary"),
            vmem_limit_bytes=VMEM_LIMIT_BYTES),
        name="proj",
    )(x2d, conv_init, *tabs, *wts, *prev_stacks)


def _attn_prompt_kernel(q_ref, k_ref, v_ref, o_ref, m_ref, acc_ref, *, tq, nq):
    i = pl.program_id(1)
    m_ref[...] = jnp.full(m_ref.shape, NEG_INF, F32)
    acc_ref[...] = jnp.zeros(acc_ref.shape, F32)
    ncol = tq // LANE
    r_chunk = lax.broadcasted_iota(jnp.int32, (tq, LANE), 0) // CHUNK
    ones = jnp.ones((tq, LANE), BF16)

    def block(j, masked):
        koff = j * tq
        for hh in range(N_HEADS):
            q = q_ref[:, hh * HEAD_PAD:(hh + 1) * HEAD_PAD]
            k = k_ref[koff:koff + tq, hh * HEAD_PAD:(hh + 1) * HEAD_PAD]
            v = v_ref[koff:koff + tq, hh * LANE:(hh + 1) * LANE]
            s = _dot_nt(q, k)
            cols = []
            for c in range(ncol):
                sc = s[:, c * LANE:(c + 1) * LANE]
                if masked:
                    c_chunk = (lax.broadcasted_iota(jnp.int32, (tq, LANE), 1) + c * LANE) // CHUNK
                    sc = jnp.where(c_chunk <= r_chunk, sc, NEG_INF)
                cols.append(sc)
            m_prev = m_ref[hh]
            m_cur = functools.reduce(jnp.maximum, cols)
            m_next = jnp.maximum(m_prev, jnp.max(m_cur, axis=1, keepdims=True))
            alpha = jnp.exp2(m_prev - m_next)
            p = jnp.concatenate([jnp.exp2(sc - m_next) for sc in cols], axis=1).astype(BF16)
            pv = _dot(p, jnp.concatenate([v, ones], axis=1))
            acc_ref[hh] = jnp.concatenate([alpha, alpha], axis=1) * acc_ref[hh] + pv
            m_ref[hh] = m_next

    for qi in range(nq):
        @pl.when(i == qi)
        def _(qi=qi):
            for j in range(qi):
                block(j, False)
            block(qi, True)

    for hh in range(N_HEADS):
        acc = acc_ref[hh]
        o_ref[:, hh * LANE:(hh + 1) * LANE] = (acc[:, :LANE] / acc[:, LANE:]).astype(BF16)


def _attn_prompt_call(q, k, v, *, b, s, tq):
    nq = s // tq
    return pl.pallas_call(
        functools.partial(_attn_prompt_kernel, tq=tq, nq=nq),
        out_shape=jax.ShapeDtypeStruct((b, s, N_HEADS * LANE), BF16),
        grid=(b, nq),
        in_specs=[pl.BlockSpec((None, tq, N_HEADS * HEAD_PAD), lambda bb, i: (bb, i, 0)),
                  pl.BlockSpec((None, s, N_HEADS * HEAD_PAD), lambda bb, i: (bb, 0, 0)),
                  pl.BlockSpec((None, s, N_HEADS * LANE), lambda bb, i: (bb, 0, 0))],
        out_specs=pl.BlockSpec((None, tq, N_HEADS * LANE), lambda bb, i: (bb, i, 0)),
        scratch_shapes=[pltpu.VMEM((N_HEADS, tq, LANE), F32),
                        pltpu.VMEM((N_HEADS, tq, 2 * LANE), F32)],
        compiler_params=pltpu.CompilerParams(
            dimension_semantics=("arbitrary", "arbitrary"),
            vmem_limit_bytes=VMEM_LIMIT_BYTES),
        name="attn_prompt",
    )(q, k, v)


def _attn_sample_kernel(q_ref, pckv_ref, pkpe_t_ref, nckv_ref, nkpe_ref, wuk_ref, wuv_ref,
                        o_ref, *, ls, rope, new_mask):
    qa, qp = [], []
    for hh in range(N_HEADS):
        qn = q_ref[:, hh * HEAD_PAD:hh * HEAD_PAD + LANE]
        qa.append(_dot_nt(qn, wuk_ref[:, hh * LANE:(hh + 1) * LANE]).astype(BF16))
        qp.append(q_ref[:, hh * HEAD_PAD + LANE:hh * HEAD_PAD + LANE + rope])
    qa = jnp.concatenate(qa, axis=0)
    qp = jnp.concatenate(qp, axis=0)
    pckv = pckv_ref[...].astype(BF16)
    nckv = nckv_ref[...].astype(BF16)
    s_past = _dot_nt(qa, pckv) + _dot(qp, pkpe_t_ref[...].astype(BF16))
    s_new = _dot_nt(qa, nckv) + _dot_nt(qp, nkpe_ref[...].astype(BF16))
    if new_mask is not None:
        r = lax.broadcasted_iota(jnp.int32, s_new.shape, 0) % ls
        c = lax.broadcasted_iota(jnp.int32, s_new.shape, 1)
        p0 = new_mask
        s_new = jnp.where((p0 + c) // CHUNK <= (p0 + r) // CHUNK, s_new, NEG_INF)
    m = jnp.maximum(jnp.max(s_past, axis=1, keepdims=True),
                    jnp.max(s_new, axis=1, keepdims=True))
    p_past = jnp.exp2(s_past - m)
    p_new = jnp.exp2(s_new - m)
    denom = jnp.sum(p_past, axis=1, keepdims=True) + jnp.sum(p_new, axis=1, keepdims=True)
    o_lat = (_dot(p_past.astype(BF16), pckv) + _dot(p_new.astype(BF16), nckv)) / denom
    o_lat = o_lat.astype(BF16)
    for hh in range(N_HEADS):
        o_ref[:, hh * LANE:(hh + 1) * LANE] = _dot(
            o_lat[hh * ls:(hh + 1) * ls], wuv_ref[:, hh * LANE:(hh + 1) * LANE]).astype(BF16)


def _attn_sample_call(q, cache_ckv, cache_kpe_t, nckv, nkpe, wuk, wuv, l, *, b, ls, new_mask):
    p, kv_lora = cache_ckv.shape[2], cache_ckv.shape[3]
    rope = cache_kpe_t.shape[2]
    bmap = lambda bb: (bb, 0, 0)
    return pl.pallas_call(
        functools.partial(_attn_sample_kernel, ls=ls, rope=rope, new_mask=new_mask),
        out_shape=jax.ShapeDtypeStruct((b, ls, N_HEADS * LANE), BF16),
        grid=(b,),
        in_specs=[pl.BlockSpec((None, ls, N_HEADS * HEAD_PAD), bmap),
                  pl.BlockSpec((None, None, p, kv_lora), lambda bb: (l, bb, 0, 0)),
                  pl.BlockSpec((None, None, rope, p), lambda bb: (l, bb, 0, 0)),
                  pl.BlockSpec((None, None, ls, kv_lora), lambda bb: (l, bb, 0, 0)),
                  pl.BlockSpec((None, None, ls, rope), lambda bb: (l, bb, 0, 0)),
                  pl.BlockSpec((None,) + wuk.shape[1:], lambda bb: (l, 0, 0)),
                  pl.BlockSpec((None,) + wuv.shape[1:], lambda bb: (l, 0, 0))],
        out_specs=pl.BlockSpec((None, ls, N_HEADS * LANE), bmap),
        compiler_params=pltpu.CompilerParams(
            dimension_semantics=("arbitrary",),
            vmem_limit_bytes=VMEM_LIMIT_BYTES),
        name="attn_sample",
    )(q, cache_ckv, cache_kpe_t, nckv, nkpe, wuk, wuv)


def _post_kernel(x_ref, ya_ref, sgb_ref, yb_ref, wo_ref, w1_ref, w2_ref,
                 gpm_ref, gpf_ref, gqf_ref, o_ref, *, d_ff, ff_chunk, n_split):
    tm = x_ref.shape[0]
    groups = [pl.ds(i * (tm // n_split), tm // n_split) for i in range(n_split)]
    n_chunks = d_ff // ff_chunk
    up = lambda h2, c: _dot(h2, w1_ref[:, c * ff_chunk:(c + 1) * ff_chunk])

    outs = []
    for r in groups:
        merged = ya_ref[r, :].astype(F32) + sgb_ref[r, :].astype(F32) * yb_ref[r, :].astype(F32)
        outs.append(_dot(merged.astype(BF16), wo_ref[...]))
    x1s = [x_ref[r, :] + _rms(o, gpm_ref[...]) for r, o in zip(groups, outs)]
    h2s = [_rms(x1, gpf_ref[...]).astype(BF16) for x1 in x1s]
    a_next = [up(h2, 0) for h2 in h2s]
    accs = [jnp.zeros(x1.shape, F32) for x1 in x1s]
    for c in range(n_chunks):
        for i, h2 in enumerate(h2s):
            a = jnp.square(jnp.maximum(a_next[i], 0.0)).astype(BF16)
            if c + 1 < n_chunks:
                a_next[i] = up(h2, c + 1)
            accs[i] = accs[i] + _dot(a, w2_ref[c * ff_chunk:(c + 1) * ff_chunk, :])
    for r, x1, acc in zip(groups, x1s, accs):
        o_ref[r, :] = x1 + _rms(acc, gqf_ref[...])


def _post_call(x2d, ya, sgb, yb, wts, l, *, tm):
    rows, d = x2d.shape
    wo, w1, w2, gpm, gpf, gqf = wts
    d_ff = w1.shape[-1]
    row_map = lambda i: (i, 0)
    lay_map = lambda i: (l, 0, 0)

    def wspec(a):
        return pl.BlockSpec((None,) + a.shape[1:], lay_map, pipeline_mode=pl.Buffered(1))

    rspec = pl.BlockSpec((tm, d), row_map)
    return pl.pallas_call(
        functools.partial(_post_kernel, d_ff=d_ff, ff_chunk=min(d_ff, 1024),
                          n_split=POST_ROW_GROUPS if tm % (POST_ROW_GROUPS * 2 * SUBLANE) == 0 else 1),
        out_shape=jax.ShapeDtypeStruct((rows, d), F32),
        grid=(rows // tm,),
        in_specs=[rspec, rspec, rspec, rspec, wspec(wo), wspec(w1), wspec(w2),
                  wspec(gpm), wspec(gpf), wspec(gqf)],
        out_specs=rspec,
        compiler_params=pltpu.CompilerParams(
            dimension_semantics=("arbitrary",),
            vmem_limit_bytes=VMEM_LIMIT_BYTES),
        name="post",
    )(x2d, ya, sgb, yb, wo, w1, w2, gpm, gpf, gqf)


def _rope_tables(pos, rope):
    inv = ROPE_THETA ** (-jnp.arange(0, rope, 2, dtype=F32) / rope)
    ang = pos.astype(F32)[:, None] * inv[None, :]
    cos, sin = jnp.cos(ang), jnp.sin(ang)
    z = jnp.zeros_like(cos)
    pad = jnp.zeros((pos.shape[0], LANE - rope), F32)
    cos_t = jnp.concatenate([cos, cos, pad], axis=1)
    sin_a = jnp.concatenate([z, sin, pad], axis=1)
    sin_b = jnp.concatenate([-sin, z, pad], axis=1)
    return cos_t, sin_a, sin_b


def _prep_weights(w_in, conv_w, w_uq, w_uk, w_uv, w_o, w1, w2, g_pre_mix, g_post_mix, g_pre_ffn,
                  g_post_ffn, g_q, g_kv, *, nope, rope):
    depth, d, _ = w_in.shape
    q_lora = w_uq.shape[1]
    off_ga = w_in.shape[-1] - 2 * d
    w_a = jnp.pad(w_in[:, :, d:off_ga],
                  ((0, 0), (0, 0), (0, LANE - rope))).astype(BF16)
    w_b = w_in[:, :, :d].astype(BF16)
    w_g = w_in[:, :, off_ga:].astype(BF16)
    uq = w_uq.reshape(depth, q_lora, N_HEADS, nope + rope)
    wuq = jnp.concatenate(
        [uq[..., :nope].reshape(depth, q_lora, N_HEADS * nope),
         uq[..., nope:].reshape(depth, q_lora, N_HEADS * rope)], axis=2).astype(BF16)
    wuk, wuv = w_uk.astype(BF16), w_uv.astype(BF16)
    g3 = lambda g: g[:, None, :]
    proj_w = (w_a, w_b, w_g, wuq, wuk, wuv, conv_w, g3(g_pre_mix), g3(g_q), g3(g_kv))
    post_w = (w_o.astype(BF16), w1.astype(BF16), w2.astype(BF16),
              g3(g_post_mix), g3(g_pre_ffn), g3(g_post_ffn))
    return proj_w, post_w, wuk, wuv


def _tile(n, pref):
    t = min(n, pref)
    assert n % t == 0, (n, t)
    return t


def kernel(x_prompt, x_sample, cache_ckv, cache_kpe, state_conv, w_in, conv_w, w_uq, w_uk, w_uv,
           w_o, w1, w2, g_pre_mix, g_post_mix, g_pre_ffn, g_post_ffn, g_q, g_kv):
    b, s, d = x_prompt.shape
    bs, ls, _ = x_sample.shape
    depth, _, p, kv_lora = cache_ckv.shape
    rope = cache_kpe.shape[-1]
    q_lora = g_q.shape[-1]
    nope = w_uk.shape[-1] // N_HEADS
    v_dim = w_uv.shape[-1] // N_HEADS
    assert nope == LANE and v_dim == LANE and rope * 2 == LANE and d == N_HEADS * v_dim
    assert conv_w.shape[1] == CONV_WIDTH and s % CHUNK == 0 and ls >= CONV_WIDTH - 1
    q_scale = float(nope + rope) ** -0.5 * float(np.log2(np.e))
    dims = (d, q_lora, kv_lora, rope, q_scale)

    assert q_lora % LANE == 0 and kv_lora % LANE == 0
    proj_w, post_w, wuk, wuv = _prep_weights(
        w_in, conv_w, w_uq, w_uk, w_uv, w_o, w1, w2, g_pre_mix, g_post_mix, g_pre_ffn,
        g_post_ffn, g_q, g_kv, nope=nope, rope=rope)
    cache_kpe_t = jnp.swapaxes(cache_kpe, 2, 3)

    tm_p = _tile(s, 512)
    tq = _tile(s, 512)
    segs_per_tile = max(1, min(bs, 512 // ls))
    assert bs % segs_per_tile == 0
    tm_s = segs_per_tile * ls
    tabs_p = _rope_tables(jnp.arange(s, dtype=jnp.int32), rope)
    tabs_s = tuple(jnp.tile(t, (segs_per_tile, 1))
                   for t in _rope_tables(p + jnp.arange(ls, dtype=jnp.int32), rope))
    pos_new = p + np.arange(ls)
    new_all_visible = bool(np.all((pos_new[None, :] // CHUNK) <= (pos_new[:, None] // CHUNK)))
    new_mask = None if new_all_visible else p

    xp = x_prompt.reshape(b * s, d)
    xs = x_sample.reshape(bs * ls, d)
    zero_buf = jnp.zeros((b, CONV_WIDTH - 1, d), F32)
    stk_p, stk_s = (), ()
    for l in range(depth):
        ya, sgb, q, *stk_p, k, v = _proj_call(
            xp, zero_buf, tabs_p, proj_w, l, stk_p, n_outer=b, n_inner=s // tm_p, tm=tm_p,
            seg=tm_p, dims=dims, expand_kv=True)
        yb = _attn_prompt_call(q.reshape(b, s, -1), k.reshape(b, s, -1), v.reshape(b, s, -1),
                               b=b, s=s, tq=tq)
        xp = _post_call(xp, ya, sgb, yb.reshape(b * s, d), post_w, l, tm=tm_p)

        ya, sgb, q, *stk_s = _proj_call(
            xs, state_conv[l], tabs_s, proj_w, l, stk_s, n_outer=bs // segs_per_tile, n_inner=1,
            tm=tm_s, seg=ls, dims=dims, expand_kv=False)
        yb = _attn_sample_call(q.reshape(bs, ls, -1), cache_ckv, cache_kpe_t,
                               stk_s[0].reshape(depth, bs, ls, kv_lora),
                               stk_s[1].reshape(depth, bs, ls, rope),
                               wuk, wuv, l, b=bs, ls=ls, new_mask=new_mask)
        xs = _post_call(xs, ya, sgb, yb.reshape(bs * ls, d), post_w, l, tm=tm_s)

    return (xp.reshape(b, s, d), xs.reshape(bs, ls, d),
            stk_p[0].reshape(depth, b, s, kv_lora), stk_p[1].reshape(depth, b, s, rope), stk_p[2],
            stk_s[0].reshape(depth, bs, ls, kv_lora), stk_s[1].reshape(depth, bs, ls, rope),
            stk_s[2])
```

```python
import functools

import numpy as np
import jax
import jax.numpy as jnp
from jax import lax
from jax.experimental import pallas as pl
from jax.experimental.pallas import tpu as pltpu

CHUNK = 64
CONV_WIDTH = 3
N_HEADS = 8
ROPE_THETA = 10000.0
EPS = 1e-6
NEG_INF = -1e30

LANE = 128
SUBLANE = 8
HEAD_PAD = 2 * LANE
VMEM_LIMIT_BYTES = 56 * 1024 * 1024
POST_ROW_GROUPS = 2
PROJ_ROW_GROUPS = 2
SAMPLE_BATCHES_PER_STEP = 2

F32 = jnp.float32
BF16 = jnp.bfloat16


def _rms(x, g):
    ms = jnp.mean(x * x, axis=-1, keepdims=True)
    return x * lax.rsqrt(ms + EPS) * g


def _rope(pe, cos_t, sin_a, sin_b):
    return (pe * cos_t + pltpu.roll(pe, LANE // 4, 1) * sin_a
            + pltpu.roll(pe, LANE - LANE // 4, 1) * sin_b)


def _dot(a, b):
    return jnp.dot(a, b, preferred_element_type=F32)


def _dot_nt(a, b):
    return lax.dot_general(a, b, (((1,), (1,)), ((), ())), preferred_element_type=F32)


def _proj_kernel(x_ref, init_ref, cos_ref, sina_ref, sinb_ref,
                 wa_ref, wb_ref, wg_ref, wuq_ref, wuk_ref, wuv_ref, convw_ref,
                 gpre_ref, gq_ref, gkv_ref, *rest,
                 d, seg, n_groups, q_lora, kv_lora, rope, sm_scale, expand_kv, n_prev):
    ya_ref, sgb_ref, q_ref, ckv_ref, kpe_ref, cbuf_ref, *rest = rest[n_prev:]
    if expand_kv:
        k_ref, v_ref, carry_ref = rest
    else:
        (carry_ref,) = rest
    inner = pl.program_id(1)
    gm = x_ref.shape[0] // n_groups
    groups = [pl.ds(g * gm, gm) for g in range(n_groups)]
    off_q = 2 * d
    off_kv = q_lora
    off_pe = off_kv + kv_lora

    @pl.when(inner == 0)
    def _():
        carry_ref[...] = init_ref[...]

    hs = [_rms(x_ref[r, :], gpre_ref[...]).astype(BF16) for r in groups]
    tabs = [(cos_ref[r, :], sina_ref[r, :], sinb_ref[r, :]) for r in groups]
    lats = [_dot(h, wa_ref[:, off_q:]) for h in hs]
    zcs = [_dot(h, wa_ref[:, :d]) for h in hs]

    for r, lat, tab in zip(groups, lats, tabs):
        cqn = _rms(lat[:, :off_kv], gq_ref[...]).astype(BF16)
        qn = _dot(cqn, wuq_ref[:, :N_HEADS * LANE]) * sm_scale
        qp = _dot(cqn, wuq_ref[:, N_HEADS * LANE:]) * sm_scale
        for hh in range(N_HEADS):
            pe = qp[:, (hh // 2) * LANE:(hh // 2 + 1) * LANE]
            if hh % 2:
                pe = pltpu.roll(pe, LANE // 2, 1)
            q_ref[r, hh * HEAD_PAD:hh * HEAD_PAD + LANE] = qn[:, hh * LANE:(hh + 1) * LANE].astype(BF16)
            q_ref[r, hh * HEAD_PAD + LANE:(hh + 1) * HEAD_PAD] = _rope(pe, *tab).astype(BF16)

    for r, lat, tab in zip(groups, lats, tabs):
        ckv = _rms(lat[:, off_kv:off_pe], gkv_ref[...])
        ckv_ref[r, :] = ckv
        kpe = _rope(lat[:, off_pe:], *tab)
        kpe_ref[r, :] = kpe[:, :rope]
        if expand_kv:
            ckvn = ckv.astype(BF16)
            kpe_b = kpe.astype(BF16)
            for hp in range(N_HEADS // 2):
                kn = _dot(ckvn, wuk_ref[:, hp * HEAD_PAD:(hp + 1) * HEAD_PAD]).astype(BF16)
                base = 2 * hp * HEAD_PAD
                k_ref[r, base:base + LANE] = kn[:, :LANE]
                k_ref[r, base + LANE:base + HEAD_PAD] = kpe_b
                k_ref[r, base + HEAD_PAD:base + HEAD_PAD + LANE] = kn[:, LANE:]
                k_ref[r, base + HEAD_PAD + LANE:base + 2 * HEAD_PAD] = kpe_b
            v_ref[r, :] = _dot(ckvn, wuv_ref[...]).astype(BF16)

    us = [zc * _dot(h, wa_ref[:, d:off_q]) for zc, h in zip(zcs, hs)]
    w0 = convw_ref[0:1, :]
    w1 = convw_ref[1:2, :]
    w2 = convw_ref[2:3, :]
    rows = lax.broadcasted_iota(jnp.int32, (SUBLANE, 1), 0)
    pm = min(seg, gm)
    ycs, tail = [], None
    for g, u in enumerate(us):
        parts = []
        for p in range(gm // pm):
            row0 = g * gm + p * pm
            sidx = row0 // seg
            up = u[p * pm:(p + 1) * pm]
            init = carry_ref[sidx] if row0 % seg == 0 else tail
            i0, i1 = init[0:1], init[1:2]
            r1, r2 = pltpu.roll(up, 1, 0), pltpu.roll(up, 2, 0)
            h1 = jnp.where(rows == 0, i1, r1[:SUBLANE])
            h2 = jnp.where(rows == 0, i0, jnp.where(rows == 1, i1, r2[:SUBLANE]))
            u1 = jnp.concatenate([h1, r1[SUBLANE:]], axis=0)
            u2 = jnp.concatenate([h2, r2[SUBLANE:]], axis=0)
            parts.append(w0 * u2 + w1 * u1 + w2 * up)
            tail = up[pm - 2:pm]
            if (row0 + pm) % seg == 0:
                carry_ref[sidx] = tail
                cbuf_ref[sidx] = tail
        ycs.append(parts[0] if len(parts) == 1 else jnp.concatenate(parts, axis=0))

    for r, h, yc in zip(groups, hs, ycs):
        ya = jax.nn.sigmoid(_dot(h, wg_ref[:, :d])) * _dot(h, wb_ref[...]) * yc
        ya_ref[r, :] = ya.astype(BF16)
    for r, h in zip(groups, hs):
        sgb_ref[r, :] = jax.nn.sigmoid(_dot(h, wg_ref[:, d:])).astype(BF16)


def _proj_call(x2d, conv_init, tabs, wts, l, prev_stacks, *, n_outer, n_inner, tm, seg, dims,
               expand_kv):
    d, q_lora, kv_lora, rope, sm_scale = dims
    depth = wts[0].shape[0]
    nseg = tm // seg
    rows = x2d.shape[0]
    row_map = lambda o, i: (o * n_inner + i, 0)
    tab_map = lambda o, i: (i, 0)
    seg_map = lambda o, i: (o, 0, 0)
    lay_map = lambda o, i: (l, 0, 0)
    lrow_map = lambda o, i: (l, o * n_inner + i, 0)

    def wspec(a):
        return pl.BlockSpec((None,) + a.shape[1:], lay_map, pipeline_mode=pl.Buffered(1))

    def rspec(n):
        return pl.BlockSpec((tm, n), row_map)

    gm = tm // PROJ_ROW_GROUPS
    n_groups = PROJ_ROW_GROUPS if gm % (2 * SUBLANE) == 0 and (gm % seg == 0 or seg % gm == 0) else 1
    kern = functools.partial(_proj_kernel, d=d, seg=seg, n_groups=n_groups, q_lora=q_lora,
                             kv_lora=kv_lora, rope=rope, sm_scale=sm_scale,
                             expand_kv=expand_kv, n_prev=len(prev_stacks))
    out_shape = [
        jax.ShapeDtypeStruct((rows, d), BF16),
        jax.ShapeDtypeStruct((rows, d), BF16),
        jax.ShapeDtypeStruct((rows, N_HEADS * HEAD_PAD), BF16),
        jax.ShapeDtypeStruct((depth, rows, kv_lora), F32),
        jax.ShapeDtypeStruct((depth, rows, rope), F32),
        jax.ShapeDtypeStruct((depth,) + conv_init.shape, F32),
    ]
    out_specs = [rspec(d), rspec(d), rspec(N_HEADS * HEAD_PAD),
                 pl.BlockSpec((None, tm, kv_lora), lrow_map),
                 pl.BlockSpec((None, tm, rope), lrow_map),
                 pl.BlockSpec((None, nseg, CONV_WIDTH - 1, d), lambda o, i: (l, o, 0, 0))]
    if expand_kv:
        out_shape += [jax.ShapeDtypeStruct((rows, N_HEADS * HEAD_PAD), BF16),
                      jax.ShapeDtypeStruct((rows, N_HEADS * LANE), BF16)]
        out_specs += [rspec(N_HEADS * HEAD_PAD), rspec(N_HEADS * LANE)]
    in_specs = [rspec(d), pl.BlockSpec((nseg, CONV_WIDTH - 1, d), seg_map),
                pl.BlockSpec((tm, LANE), tab_map), pl.BlockSpec((tm, LANE), tab_map),
                pl.BlockSpec((tm, LANE), tab_map),
                *[wspec(w) for w in wts]]
    n_in = len(in_specs)
    in_specs += [pl.BlockSpec(memory_space=pl.ANY)] * len(prev_stacks)
    stack_out0 = 3
    return pl.pallas_call(
        kern,
        out_shape=out_shape,
        grid=(n_outer, n_inner),
        in_specs=in_specs,
        out_specs=out_specs,
        scratch_shapes=[pltpu.VMEM((nseg, CONV_WIDTH - 1, d), F32)],
        input_output_aliases={n_in + j: stack_out0 + j for j in range(len(prev_stacks))},
        compiler_params=pltpu.CompilerParams(
            dimension_semantics=("arbitrary", "arbitrary"),
            vmem_limit_bytes=VMEM_LIMIT_BYTES),
        name="proj",
    )(x2d, conv_init, *tabs, *wts, *prev_stacks)


def _attn_prompt_kernel(q_ref, k_ref, v_ref, o_ref, *, tq, nq):
    i = pl.program_id(1)
    ncol = tq // LANE
    r_chunk = lax.broadcasted_iota(jnp.int32, (tq, LANE), 0) // CHUNK
    ones = jnp.ones((tq, LANE), BF16)

    def head_block(j, hh, masked, state):
        koff = j * tq
        q = q_ref[:, hh * HEAD_PAD:(hh + 1) * HEAD_PAD]
        k = k_ref[koff:koff + tq, hh * HEAD_PAD:(hh + 1) * HEAD_PAD]
        v = v_ref[koff:koff + tq, hh * LANE:(hh + 1) * LANE]
        s = _dot_nt(q, k)
        cols = []
        for c in range(ncol):
            sc = s[:, c * LANE:(c + 1) * LANE]
            if masked:
                c_chunk = (lax.broadcasted_iota(jnp.int32, (tq, LANE), 1) + c * LANE) // CHUNK
                sc = jnp.where(c_chunk <= r_chunk, sc, NEG_INF)
            cols.append(sc)
        m_cur = jnp.max(functools.reduce(jnp.maximum, cols), axis=1, keepdims=True)
        m_next = (jnp.broadcast_to(m_cur, (tq, LANE)) if state is None
                  else jnp.maximum(state[0], m_cur))
        p = jnp.concatenate([jnp.exp2(sc - m_next) for sc in cols], axis=1).astype(BF16)
        pv = _dot(p, jnp.concatenate([v, ones], axis=1))
        if state is None:
            return m_next, pv
        alpha = jnp.exp2(state[0] - m_next)
        return m_next, jnp.concatenate([alpha, alpha], axis=1) * state[1] + pv

    for qi in range(nq):
        @pl.when(i == qi)
        def _(qi=qi):
            state = [None] * N_HEADS
            for j in range(qi + 1):
                for hh in range(N_HEADS):
                    state[hh] = head_block(j, hh, j == qi, state[hh])
            for hh in range(N_HEADS):
                acc = state[hh][1]
                o_ref[:, hh * LANE:(hh + 1) * LANE] = (acc[:, :LANE] / acc[:, LANE:]).astype(BF16)


def _attn_prompt_call(q, k, v, *, b, s, tq):
    nq = s // tq
    return pl.pallas_call(
        functools.partial(_attn_prompt_kernel, tq=tq, nq=nq),
        out_shape=jax.ShapeDtypeStruct((b, s, N_HEADS * LANE), BF16),
        grid=(b, nq),
        in_specs=[pl.BlockSpec((None, tq, N_HEADS * HEAD_PAD), lambda bb, i: (bb, i, 0)),
                  pl.BlockSpec((None, s, N_HEADS * HEAD_PAD), lambda bb, i: (bb, 0, 0)),
                  pl.BlockSpec((None, s, N_HEADS * LANE), lambda bb, i: (bb, 0, 0))],
        out_specs=pl.BlockSpec((None, tq, N_HEADS * LANE), lambda bb, i: (bb, i, 0)),
        compiler_params=pltpu.CompilerParams(
            dimension_semantics=("arbitrary", "arbitrary"),
            vmem_limit_bytes=VMEM_LIMIT_BYTES),
        name="attn_prompt",
    )(q, k, v)


def _attn_sample_kernel(q_ref, pckv_ref, pkpe_t_ref, nckv_ref, nkpe_ref, wuk_ref, wuv_ref,
                        o_ref, *, ls, rope, new_mask, nb):
    batches = range(nb)
    qas, qps = [], []
    for bi in batches:
        qa, qp = [], []
        for hh in range(N_HEADS):
            qn = q_ref[bi, :, hh * HEAD_PAD:hh * HEAD_PAD + LANE]
            qa.append(_dot_nt(qn, wuk_ref[:, hh * LANE:(hh + 1) * LANE]).astype(BF16))
            qp.append(q_ref[bi, :, hh * HEAD_PAD + LANE:hh * HEAD_PAD + LANE + rope])
        qas.append(jnp.concatenate(qa, axis=0))
        qps.append(jnp.concatenate(qp, axis=0))
    pckvs = [pckv_ref[bi].astype(BF16) for bi in batches]
    nckvs = [nckv_ref[bi].astype(BF16) for bi in batches]
    scores = []
    for bi in batches:
        s_past = _dot_nt(qas[bi], pckvs[bi]) + _dot(qps[bi], pkpe_t_ref[bi].astype(BF16))
        s_new = _dot_nt(qas[bi], nckvs[bi]) + _dot_nt(qps[bi], nkpe_ref[bi].astype(BF16))
        if new_mask is not None:
            r = lax.broadcasted_iota(jnp.int32, s_new.shape, 0) % ls
            c = lax.broadcasted_iota(jnp.int32, s_new.shape, 1)
            p0 = new_mask
            s_new = jnp.where((p0 + c) // CHUNK <= (p0 + r) // CHUNK, s_new, NEG_INF)
        scores.append((s_past, s_new))
    o_lats = []
    for bi, (s_past, s_new) in enumerate(scores):
        m = jnp.maximum(jnp.max(s_past, axis=1, keepdims=True),
                        jnp.max(s_new, axis=1, keepdims=True))
        p_past = jnp.exp2(s_past - m)
        p_new = jnp.exp2(s_new - m)
        denom = jnp.sum(p_past, axis=1, keepdims=True) + jnp.sum(p_new, axis=1, keepdims=True)
        o_lat = (_dot(p_past.astype(BF16), pckvs[bi]) + _dot(p_new.astype(BF16), nckvs[bi])) / denom
        o_lats.append(o_lat.astype(BF16))
    for bi, o_lat in enumerate(o_lats):
        for hh in range(N_HEADS):
            o_ref[bi, :, hh * LANE:(hh + 1) * LANE] = _dot(
                o_lat[hh * ls:(hh + 1) * ls], wuv_ref[:, hh * LANE:(hh + 1) * LANE]).astype(BF16)


def _attn_sample_call(q, cache_ckv, cache_kpe_t, nckv, nkpe, wuk, wuv, l, *, b, ls, new_mask):
    p, kv_lora = cache_ckv.shape[2], cache_ckv.shape[3]
    rope = cache_kpe_t.shape[2]
    nb = SAMPLE_BATCHES_PER_STEP if b % SAMPLE_BATCHES_PER_STEP == 0 else 1
    bmap = lambda bb: (bb, 0, 0)
    lmap = lambda bb: (l, bb, 0, 0)
    return pl.pallas_call(
        functools.partial(_attn_sample_kernel, ls=ls, rope=rope, new_mask=new_mask, nb=nb),
        out_shape=jax.ShapeDtypeStruct((b, ls, N_HEADS * LANE), BF16),
        grid=(b // nb,),
        in_specs=[pl.BlockSpec((nb, ls, N_HEADS * HEAD_PAD), bmap),
                  pl.BlockSpec((None, nb, p, kv_lora), lmap),
                  pl.BlockSpec((None, nb, rope, p), lmap),
                  pl.BlockSpec((None, nb, ls, kv_lora), lmap),
                  pl.BlockSpec((None, nb, ls, rope), lmap),
                  pl.BlockSpec((None,) + wuk.shape[1:], lambda bb: (l, 0, 0)),
                  pl.BlockSpec((None,) + wuv.shape[1:], lambda bb: (l, 0, 0))],
        out_specs=pl.BlockSpec((nb, ls, N_HEADS * LANE), bmap),
        compiler_params=pltpu.CompilerParams(
            dimension_semantics=("arbitrary",),
            vmem_limit_bytes=VMEM_LIMIT_BYTES),
        name="attn_sample",
    )(q, cache_ckv, cache_kpe_t, nckv, nkpe, wuk, wuv)


def _post_kernel(x_ref, ya_ref, sgb_ref, yb_ref, wo_ref, w1_ref, w2_ref,
                 gpm_ref, gpf_ref, gqf_ref, o_ref, *, d_ff, ff_chunk, n_split):
    tm = x_ref.shape[0]
    groups = [pl.ds(i * (tm // n_split), tm // n_split) for i in range(n_split)]
    n_chunks = d_ff // ff_chunk
    up = lambda h2, c: _dot(h2, w1_ref[:, c * ff_chunk:(c + 1) * ff_chunk])

    outs = []
    for r in groups:
        merged = ya_ref[r, :].astype(F32) + sgb_ref[r, :].astype(F32) * yb_ref[r, :].astype(F32)
        outs.append(_dot(merged.astype(BF16), wo_ref[...]))
    x1s = [x_ref[r, :] + _rms(o, gpm_ref[...]) for r, o in zip(groups, outs)]
    h2s = [_rms(x1, gpf_ref[...]).astype(BF16) for x1 in x1s]
    a_next = [up(h2, 0) for h2 in h2s]
    accs = [jnp.zeros(x1.shape, F32) for x1 in x1s]
    for c in range(n_chunks):
        for i, h2 in enumerate(h2s):
            a = jnp.square(jnp.maximum(a_next[i], 0.0)).astype(BF16)
            if c + 1 < n_chunks:
                a_next[i] = up(h2, c + 1)
            accs[i] = accs[i] + _dot(a, w2_ref[c * ff_chunk:(c + 1) * ff_chunk, :])
    for r, x1, acc in zip(groups, x1s, accs):
        o_ref[r, :] = x1 + _rms(acc, gqf_ref[...])


def _post_call(x2d, ya, sgb, yb, wts, l, *, tm):
    rows, d = x2d.shape
    wo, w1, w2, gpm, gpf, gqf = wts
    d_ff = w1.shape[-1]
    row_map = lambda i: (i, 0)
    lay_map = lambda i: (l, 0, 0)

    def wspec(a):
        return pl.BlockSpec((None,) + a.shape[1:], lay_map, pipeline_mode=pl.Buffered(1))

    rspec = pl.BlockSpec((tm, d), row_map)
    return pl.pallas_call(
        functools.partial(_post_kernel, d_ff=d_ff, ff_chunk=min(d_ff, 1024),
                          n_split=POST_ROW_GROUPS if tm % (POST_ROW_GROUPS * 2 * SUBLANE) == 0 else 1),
        out_shape=jax.ShapeDtypeStruct((rows, d), F32),
        grid=(rows // tm,),
        in_specs=[rspec, rspec, rspec, rspec, wspec(wo), wspec(w1), wspec(w2),
                  wspec(gpm), wspec(gpf), wspec(gqf)],
        out_specs=rspec,
        compiler_params=pltpu.CompilerParams(
            dimension_semantics=("arbitrary",),
            vmem_limit_bytes=VMEM_LIMIT_BYTES),
        name="post",
    )(x2d, ya, sgb, yb, wo, w1, w2, gpm, gpf, gqf)


def _rope_tables(pos, rope):
    inv = ROPE_THETA ** (-jnp.arange(0, rope, 2, dtype=F32) / rope)
    ang = pos.astype(F32)[:, None] * inv[None, :]
    cos, sin = jnp.cos(ang), jnp.sin(ang)
    z = jnp.zeros_like(cos)
    pad = jnp.zeros((pos.shape[0], LANE - rope), F32)
    cos_t = jnp.concatenate([cos, cos, pad], axis=1)
    sin_a = jnp.concatenate([z, sin, pad], axis=1)
    sin_b = jnp.concatenate([-sin, z, pad], axis=1)
    return cos_t, sin_a, sin_b


def _prep_weights(w_in, conv_w, w_uq, w_uk, w_uv, w_o, w1, w2, g_pre_mix, g_post_mix, g_pre_ffn,
                  g_post_ffn, g_q, g_kv, *, nope, rope):
    depth, d, _ = w_in.shape
    q_lora = w_uq.shape[1]
    off_ga = w_in.shape[-1] - 2 * d
    w_a = jnp.pad(w_in[:, :, d:off_ga],
                  ((0, 0), (0, 0), (0, LANE - rope))).astype(BF16)
    w_b = w_in[:, :, :d].astype(BF16)
    w_g = w_in[:, :, off_ga:].astype(BF16)
    uq = w_uq.reshape(depth, q_lora, N_HEADS, nope + rope)
    wuq = jnp.concatenate(
        [uq[..., :nope].reshape(depth, q_lora, N_HEADS * nope),
         uq[..., nope:].reshape(depth, q_lora, N_HEADS * rope)], axis=2).astype(BF16)
    wuk, wuv = w_uk.astype(BF16), w_uv.astype(BF16)
    g3 = lambda g: g[:, None, :]
    proj_w = (w_a, w_b, w_g, wuq, wuk, wuv, conv_w, g3(g_pre_mix), g3(g_q), g3(g_kv))
    post_w = (w_o.astype(BF16), w1.astype(BF16), w2.astype(BF16),
              g3(g_post_mix), g3(g_pre_ffn), g3(g_post_ffn))
    return proj_w, post_w, wuk, wuv


def _tile(n, pref):
    t = min(n, pref)
    assert n % t == 0, (n, t)
    return t


def kernel(x_prompt, x_sample, cache_ckv, cache_kpe, state_conv, w_in, conv_w, w_uq, w_uk, w_uv,
           w_o, w1, w2, g_pre_mix, g_post_mix, g_pre_ffn, g_post_ffn, g_q, g_kv):
    b, s, d = x_prompt.shape
    bs, ls, _ = x_sample.shape
    depth, _, p, kv_lora = cache_ckv.shape
    rope = cache_kpe.shape[-1]
    q_lora = g_q.shape[-1]
    nope = w_uk.shape[-1] // N_HEADS
    v_dim = w_uv.shape[-1] // N_HEADS
    assert nope == LANE and v_dim == LANE and rope * 2 == LANE and d == N_HEADS * v_dim
    assert conv_w.shape[1] == CONV_WIDTH and s % CHUNK == 0 and ls >= CONV_WIDTH - 1
    q_scale = float(nope + rope) ** -0.5 * float(np.log2(np.e))
    dims = (d, q_lora, kv_lora, rope, q_scale)

    assert q_lora % LANE == 0 and kv_lora % LANE == 0
    proj_w, post_w, wuk, wuv = _prep_weights(
        w_in, conv_w, w_uq, w_uk, w_uv, w_o, w1, w2, g_pre_mix, g_post_mix, g_pre_ffn,
        g_post_ffn, g_q, g_kv, nope=nope, rope=rope)
    cache_kpe_t = jnp.swapaxes(cache_kpe, 2, 3)

    tm_p = _tile(s, 512)
    tq = _tile(s, 512)
    segs_per_tile = max(1, min(bs, 512 // ls))
    assert bs % segs_per_tile == 0
    tm_s = segs_per_tile * ls
    tabs_p = _rope_tables(jnp.arange(s, dtype=jnp.int32), rope)
    tabs_s = tuple(jnp.tile(t, (segs_per_tile, 1))
                   for t in _rope_tables(p + jnp.arange(ls, dtype=jnp.int32), rope))
    pos_new = p + np.arange(ls)
    new_all_visible = bool(np.all((pos_new[None, :] // CHUNK) <= (pos_new[:, None] // CHUNK)))
    new_mask = None if new_all_visible else p

    xp = x_prompt.reshape(b * s, d)
    xs = x_sample.reshape(bs * ls, d)
    zero_buf = jnp.zeros((b, CONV_WIDTH - 1, d), F32)
    stk_p, stk_s = (), ()
    for l in range(depth):
        ya, sgb, q, *stk_p, k, v = _proj_call(
            xp, zero_buf, tabs_p, proj_w, l, stk_p, n_outer=b, n_inner=s // tm_p, tm=tm_p,
            seg=tm_p, dims=dims, expand_kv=True)
        yb = _attn_prompt_call(q.reshape(b, s, -1), k.reshape(b, s, -1), v.reshape(b, s, -1),
                               b=b, s=s, tq=tq)
        xp = _post_call(xp, ya, sgb, yb.reshape(b * s, d), post_w, l, tm=tm_p)

        ya, sgb, q, *stk_s = _proj_call(
            xs, state_conv[l], tabs_s, proj_w, l, stk_s, n_outer=bs // segs_per_tile, n_inner=1,
            tm=tm_s, seg=ls, dims=dims, expand_kv=False)
        yb = _attn_sample_call(q.reshape(bs, ls, -1), cache_ckv, cache_kpe_t,
                               stk_s[0].reshape(depth, bs, ls, kv_lora),
                               stk_s[1].reshape(depth, bs, ls, rope),
                               wuk, wuv, l, b=bs, ls=ls, new_mask=new_mask)
        xs = _post_call(xs, ya, sgb, yb.reshape(bs * ls, d), post_w, l, tm=tm_s)

    return (xp.reshape(b, s, d), xs.reshape(bs, ls, d),
            stk_p[0].reshape(depth, b, s, kv_lora), stk_p[1].reshape(depth, b, s, rope), stk_p[2],
            stk_s[0].reshape(depth, bs, ls, kv_lora), stk_s[1].reshape(depth, bs, ls, rope),
            stk_s[2])
```

```python
import functools

import numpy as np
import jax
import jax.numpy as jnp
from jax import lax
from jax.experimental import pallas as pl
from jax.experimental.pallas import tpu as pltpu

CHUNK = 64
CONV_WIDTH = 3
N_HEADS = 8
ROPE_THETA = 10000.0
EPS = 1e-6
NEG_INF = -1e30

LANE = 128
SUBLANE = 8
HEAD_PAD = 2 * LANE
VMEM_LIMIT_BYTES = 56 * 1024 * 1024
POST_ROW_GROUPS = 2
PROJ_ROW_GROUPS = 2
SAMPLE_BATCHES_PER_STEP = 2
ATTN_LOOKAHEAD = 2

F32 = jnp.float32
BF16 = jnp.bfloat16


def _rms(x, g):
    ms = jnp.mean(x * x, axis=-1, keepdims=True)
    return x * lax.rsqrt(ms + EPS) * g


def _rope(pe, cos_t, sin_a, sin_b):
    return (pe * cos_t + pltpu.roll(pe, LANE // 4, 1) * sin_a
            + pltpu.roll(pe, LANE - LANE // 4, 1) * sin_b)


def _dot(a, b):
    return jnp.dot(a, b, preferred_element_type=F32)


def _dot_nt(a, b):
    return lax.dot_general(a, b, (((1,), (1,)), ((), ())), preferred_element_type=F32)


def _proj_kernel(x_ref, init_ref, cos_ref, sina_ref, sinb_ref,
                 wa_ref, wb_ref, wg_ref, wuq_ref, wuk_ref, wuv_ref, convw_ref,
                 gpre_ref, gq_ref, gkv_ref, *rest,
                 d, seg, n_groups, q_lora, kv_lora, rope, sm_scale, expand_kv, n_prev):
    ya_ref, sgb_ref, q_ref, ckv_ref, kpe_ref, cbuf_ref, *rest = rest[n_prev:]
    if expand_kv:
        k_ref, v_ref, carry_ref = rest
    else:
        (carry_ref,) = rest
    inner = pl.program_id(1)
    gm = x_ref.shape[0] // n_groups
    groups = [pl.ds(g * gm, gm) for g in range(n_groups)]
    off_q = 2 * d
    off_kv = q_lora
    off_pe = off_kv + kv_lora

    @pl.when(inner == 0)
    def _():
        carry_ref[...] = init_ref[...]

    hs = [_rms(x_ref[r, :], gpre_ref[...]).astype(BF16) for r in groups]
    tabs = [(cos_ref[r, :], sina_ref[r, :], sinb_ref[r, :]) for r in groups]
    lats = [_dot(h, wa_ref[:, off_q:]) for h in hs]
    zcs = [_dot(h, wa_ref[:, :d]) for h in hs]

    for r, lat, tab in zip(groups, lats, tabs):
        cqn = _rms(lat[:, :off_kv], gq_ref[...]).astype(BF16)
        qn = _dot(cqn, wuq_ref[:, :N_HEADS * LANE]) * sm_scale
        qp = _dot(cqn, wuq_ref[:, N_HEADS * LANE:]) * sm_scale
        for hh in range(N_HEADS):
            pe = qp[:, (hh // 2) * LANE:(hh // 2 + 1) * LANE]
            if hh % 2:
                pe = pltpu.roll(pe, LANE // 2, 1)
            q_ref[r, hh * HEAD_PAD:hh * HEAD_PAD + LANE] = qn[:, hh * LANE:(hh + 1) * LANE].astype(BF16)
            q_ref[r, hh * HEAD_PAD + LANE:(hh + 1) * HEAD_PAD] = _rope(pe, *tab).astype(BF16)

    for r, lat, tab in zip(groups, lats, tabs):
        ckv = _rms(lat[:, off_kv:off_pe], gkv_ref[...])
        ckv_ref[r, :] = ckv
        kpe = _rope(lat[:, off_pe:], *tab)
        kpe_ref[r, :] = kpe[:, :rope]
        if expand_kv:
            ckvn = ckv.astype(BF16)
            kpe_b = kpe.astype(BF16)
            for hp in range(N_HEADS // 2):
                kn = _dot(ckvn, wuk_ref[:, hp * HEAD_PAD:(hp + 1) * HEAD_PAD]).astype(BF16)
                base = 2 * hp * HEAD_PAD
                k_ref[r, base:base + LANE] = kn[:, :LANE]
                k_ref[r, base + LANE:base + HEAD_PAD] = kpe_b
                k_ref[r, base + HEAD_PAD:base + HEAD_PAD + LANE] = kn[:, LANE:]
                k_ref[r, base + HEAD_PAD + LANE:base + 2 * HEAD_PAD] = kpe_b
            v_ref[r, :] = _dot(ckvn, wuv_ref[...]).astype(BF16)

    us = [zc * _dot(h, wa_ref[:, d:off_q]) for zc, h in zip(zcs, hs)]
    w0 = convw_ref[0:1, :]
    w1 = convw_ref[1:2, :]
    w2 = convw_ref[2:3, :]
    rows = lax.broadcasted_iota(jnp.int32, (SUBLANE, 1), 0)
    pm = min(seg, gm)
    ycs, tail = [], None
    for g, u in enumerate(us):
        parts = []
        for p in range(gm // pm):
            row0 = g * gm + p * pm
            sidx = row0 // seg
            up = u[p * pm:(p + 1) * pm]
            init = carry_ref[sidx] if row0 % seg == 0 else tail
            i0, i1 = init[0:1], init[1:2]
            r1, r2 = pltpu.roll(up, 1, 0), pltpu.roll(up, 2, 0)
            h1 = jnp.where(rows == 0, i1, r1[:SUBLANE])
            h2 = jnp.where(rows == 0, i0, jnp.where(rows == 1, i1, r2[:SUBLANE]))
            u1 = jnp.concatenate([h1, r1[SUBLANE:]], axis=0)
            u2 = jnp.concatenate([h2, r2[SUBLANE:]], axis=0)
            parts.append(w0 * u2 + w1 * u1 + w2 * up)
            tail = up[pm - 2:pm]
            if (row0 + pm) % seg == 0:
                carry_ref[sidx] = tail
                cbuf_ref[sidx] = tail
        ycs.append(parts[0] if len(parts) == 1 else jnp.concatenate(parts, axis=0))

    for r, h, yc in zip(groups, hs, ycs):
        ya = jax.nn.sigmoid(_dot(h, wg_ref[:, :d])) * _dot(h, wb_ref[...]) * yc
        ya_ref[r, :] = ya.astype(BF16)
    for r, h in zip(groups, hs):
        sgb_ref[r, :] = jax.nn.sigmoid(_dot(h, wg_ref[:, d:])).astype(BF16)


def _proj_call(x2d, conv_init, tabs, wts, l, prev_stacks, *, n_outer, n_inner, tm, seg, dims,
               expand_kv):
    d, q_lora, kv_lora, rope, sm_scale = dims
    depth = wts[0].shape[0]
    nseg = tm // seg
    rows = x2d.shape[0]
    row_map = lambda o, i: (o * n_inner + i, 0)
    tab_map = lambda o, i: (i, 0)
    seg_map = lambda o, i: (o, 0, 0)
    lay_map = lambda o, i: (l, 0, 0)
    lrow_map = lambda o, i: (l, o * n_inner + i, 0)

    def wspec(a):
        return pl.BlockSpec((None,) + a.shape[1:], lay_map, pipeline_mode=pl.Buffered(1))

    def rspec(n):
        return pl.BlockSpec((tm, n), row_map)

    gm = tm // PROJ_ROW_GROUPS
    n_groups = PROJ_ROW_GROUPS if gm % (2 * SUBLANE) == 0 and (gm % seg == 0 or seg % gm == 0) else 1
    kern = functools.partial(_proj_kernel, d=d, seg=seg, n_groups=n_groups, q_lora=q_lora,
                             kv_lora=kv_lora, rope=rope, sm_scale=sm_scale,
                             expand_kv=expand_kv, n_prev=len(prev_stacks))
    out_shape = [
        jax.ShapeDtypeStruct((rows, d), BF16),
        jax.ShapeDtypeStruct((rows, d), BF16),
        jax.ShapeDtypeStruct((rows, N_HEADS * HEAD_PAD), BF16),
        jax.ShapeDtypeStruct((depth, rows, kv_lora), F32),
        jax.ShapeDtypeStruct((depth, rows, rope), F32),
        jax.ShapeDtypeStruct((depth,) + conv_init.shape, F32),
    ]
    out_specs = [rspec(d), rspec(d), rspec(N_HEADS * HEAD_PAD),
                 pl.BlockSpec((None, tm, kv_lora), lrow_map),
                 pl.BlockSpec((None, tm, rope), lrow_map),
                 pl.BlockSpec((None, nseg, CONV_WIDTH - 1, d), lambda o, i: (l, o, 0, 0))]
    if expand_kv:
        out_shape += [jax.ShapeDtypeStruct((rows, N_HEADS * HEAD_PAD), BF16),
                      jax.ShapeDtypeStruct((rows, N_HEADS * LANE), BF16)]
        out_specs += [rspec(N_HEADS * HEAD_PAD), rspec(N_HEADS * LANE)]
    in_specs = [rspec(d), pl.BlockSpec((nseg, CONV_WIDTH - 1, d), seg_map),
                pl.BlockSpec((tm, LANE), tab_map), pl.BlockSpec((tm, LANE), tab_map),
                pl.BlockSpec((tm, LANE), tab_map),
                *[wspec(w) for w in wts]]
    n_in = len(in_specs)
    in_specs += [pl.BlockSpec(memory_space=pl.ANY)] * len(prev_stacks)
    stack_out0 = 3
    return pl.pallas_call(
        kern,
        out_shape=out_shape,
        grid=(n_outer, n_inner),
        in_specs=in_specs,
        out_specs=out_specs,
        scratch_shapes=[pltpu.VMEM((nseg, CONV_WIDTH - 1, d), F32)],
        input_output_aliases={n_in + j: stack_out0 + j for j in range(len(prev_stacks))},
        compiler_params=pltpu.CompilerParams(
            dimension_semantics=("arbitrary", "arbitrary"),
            vmem_limit_bytes=VMEM_LIMIT_BYTES),
        name="proj",
    )(x2d, conv_init, *tabs, *wts, *prev_stacks)


def _attn_prompt_kernel(q_ref, k_ref, v_ref, o_ref, *, tq, nq):
    i = pl.program_id(1)
    ncol = tq // LANE
    r_chunk = lax.broadcasted_iota(jnp.int32, (tq, LANE), 0) // CHUNK
    ones = jnp.ones((tq, LANE), BF16)

    def scores(j, hh, masked):
        q = q_ref[:, hh * HEAD_PAD:(hh + 1) * HEAD_PAD]
        k = k_ref[j * tq:(j + 1) * tq, hh * HEAD_PAD:(hh + 1) * HEAD_PAD]
        s = _dot_nt(q, k)
        cols = []
        for c in range(ncol):
            sc = s[:, c * LANE:(c + 1) * LANE]
            if masked:
                c_chunk = (lax.broadcasted_iota(jnp.int32, (tq, LANE), 1) + c * LANE) // CHUNK
                sc = jnp.where(c_chunk <= r_chunk, sc, NEG_INF)
            cols.append(sc)
        return cols

    def accumulate(j, hh, cols, state):
        v = v_ref[j * tq:(j + 1) * tq, hh * LANE:(hh + 1) * LANE]
        m_cur = jnp.max(functools.reduce(jnp.maximum, cols), axis=1, keepdims=True)
        m_next = (jnp.broadcast_to(m_cur, (tq, LANE)) if state is None
                  else jnp.maximum(state[0], m_cur))
        p = jnp.concatenate([jnp.exp2(sc - m_next) for sc in cols], axis=1).astype(BF16)
        pv = _dot(p, jnp.concatenate([v, ones], axis=1))
        if state is None:
            return m_next, pv
        alpha = jnp.exp2(state[0] - m_next)
        return m_next, jnp.concatenate([alpha, alpha], axis=1) * state[1] + pv

    for qi in range(nq):
        @pl.when(i == qi)
        def _(qi=qi):
            state = [None] * N_HEADS
            units = [(j, hh, j == qi) for j in range(qi + 1) for hh in range(N_HEADS)]
            pending = {t: scores(*units[t]) for t in range(min(ATTN_LOOKAHEAD, len(units)))}
            for t, (j, hh, _) in enumerate(units):
                if t + ATTN_LOOKAHEAD < len(units):
                    pending[t + ATTN_LOOKAHEAD] = scores(*units[t + ATTN_LOOKAHEAD])
                state[hh] = accumulate(j, hh, pending.pop(t), state[hh])
            for hh in range(N_HEADS):
                acc = state[hh][1]
                o_ref[:, hh * LANE:(hh + 1) * LANE] = (acc[:, :LANE] / acc[:, LANE:]).astype(BF16)


def _attn_prompt_call(q, k, v, *, b, s, tq):
    nq = s // tq
    return pl.pallas_call(
        functools.partial(_attn_prompt_kernel, tq=tq, nq=nq),
        out_shape=jax.ShapeDtypeStruct((b, s, N_HEADS * LANE), BF16),
        grid=(b, nq),
        in_specs=[pl.BlockSpec((None, tq, N_HEADS * HEAD_PAD), lambda bb, i: (bb, i, 0)),
                  pl.BlockSpec((None, s, N_HEADS * HEAD_PAD), lambda bb, i: (bb, 0, 0)),
                  pl.BlockSpec((None, s, N_HEADS * LANE), lambda bb, i: (bb, 0, 0))],
        out_specs=pl.BlockSpec((None, tq, N_HEADS * LANE), lambda bb, i: (bb, i, 0)),
        compiler_params=pltpu.CompilerParams(
            dimension_semantics=("arbitrary", "arbitrary"),
            vmem_limit_bytes=VMEM_LIMIT_BYTES),
        name="attn_prompt",
    )(q, k, v)


def _attn_sample_kernel(q_ref, pckv_ref, pkpe_t_ref, nckv_ref, nkpe_ref, wuk_ref, wuv_ref,
                        o_ref, *, ls, rope, new_mask, nb):
    batches = range(nb)
    qas, qps = [], []
    for bi in batches:
        qa, qp = [], []
        for hh in range(N_HEADS):
            qn = q_ref[bi, :, hh * HEAD_PAD:hh * HEAD_PAD + LANE]
            qa.append(_dot_nt(qn, wuk_ref[:, hh * LANE:(hh + 1) * LANE]).astype(BF16))
            qp.append(q_ref[bi, :, hh * HEAD_PAD + LANE:hh * HEAD_PAD + LANE + rope])
        qas.append(jnp.concatenate(qa, axis=0))
        qps.append(jnp.concatenate(qp, axis=0))
    pckvs = [pckv_ref[bi].astype(BF16) for bi in batches]
    nckvs = [nckv_ref[bi].astype(BF16) for bi in batches]
    scores = []
    for bi in batches:
        s_past = _dot_nt(qas[bi], pckvs[bi]) + _dot(qps[bi], pkpe_t_ref[bi].astype(BF16))
        s_new = _dot_nt(qas[bi], nckvs[bi]) + _dot_nt(qps[bi], nkpe_ref[bi].astype(BF16))
        if new_mask is not None:
            r = lax.broadcasted_iota(jnp.int32, s_new.shape, 0) % ls
            c = lax.broadcasted_iota(jnp.int32, s_new.shape, 1)
            p0 = new_mask
            s_new = jnp.where((p0 + c) // CHUNK <= (p0 + r) // CHUNK, s_new, NEG_INF)
        scores.append((s_past, s_new))
    o_lats = []
    for bi, (s_past, s_new) in enumerate(scores):
        m = jnp.maximum(jnp.max(s_past, axis=1, keepdims=True),
                        jnp.max(s_new, axis=1, keepdims=True))
        p_past = jnp.exp2(s_past - m)
        p_new = jnp.exp2(s_new - m)
        denom = jnp.sum(p_past, axis=1, keepdims=True) + jnp.sum(p_new, axis=1, keepdims=True)
        o_lat = (_dot(p_past.astype(BF16), pckvs[bi]) + _dot(p_new.astype(BF16), nckvs[bi])) / denom
        o_lats.append(o_lat.astype(BF16))
    for bi, o_lat in enumerate(o_lats):
        for hh in range(N_HEADS):
            o_ref[bi, :, hh * LANE:(hh + 1) * LANE] = _dot(
                o_lat[hh * ls:(hh + 1) * ls], wuv_ref[:, hh * LANE:(hh + 1) * LANE]).astype(BF16)


def _attn_sample_call(q, cache_ckv, cache_kpe_t, nckv, nkpe, wuk, wuv, l, *, b, ls, new_mask):
    p, kv_lora = cache_ckv.shape[2], cache_ckv.shape[3]
    rope = cache_kpe_t.shape[2]
    nb = SAMPLE_BATCHES_PER_STEP if b % SAMPLE_BATCHES_PER_STEP == 0 else 1
    bmap = lambda bb: (bb, 0, 0)
    lmap = lambda bb: (l, bb, 0, 0)
    return pl.pallas_call(
        functools.partial(_attn_sample_kernel, ls=ls, rope=rope, new_mask=new_mask, nb=nb),
        out_shape=jax.ShapeDtypeStruct((b, ls, N_HEADS * LANE), BF16),
        grid=(b // nb,),
        in_specs=[pl.BlockSpec((nb, ls, N_HEADS * HEAD_PAD), bmap),
                  pl.BlockSpec((None, nb, p, kv_lora), lmap),
                  pl.BlockSpec((None, nb, rope, p), lmap),
                  pl.BlockSpec((None, nb, ls, kv_lora), lmap),
                  pl.BlockSpec((None, nb, ls, rope), lmap),
                  pl.BlockSpec((None,) + wuk.shape[1:], lambda bb: (l, 0, 0)),
                  pl.BlockSpec((None,) + wuv.shape[1:], lambda bb: (l, 0, 0))],
        out_specs=pl.BlockSpec((nb, ls, N_HEADS * LANE), bmap),
        compiler_params=pltpu.CompilerParams(
            dimension_semantics=("arbitrary",),
            vmem_limit_bytes=VMEM_LIMIT_BYTES),
        name="attn_sample",
    )(q, cache_ckv, cache_kpe_t, nckv, nkpe, wuk, wuv)


def _post_kernel(x_ref, ya_ref, sgb_ref, yb_ref, wo_ref, w1_ref, w2_ref,
                 gpm_ref, gpf_ref, gqf_ref, o_ref, *, d_ff, ff_chunk, n_split):
    tm = x_ref.shape[0]
    groups = [pl.ds(i * (tm // n_split), tm // n_split) for i in range(n_split)]
    n_chunks = d_ff // ff_chunk
    up = lambda h2, c: _dot(h2, w1_ref[:, c * ff_chunk:(c + 1) * ff_chunk])

    outs = []
    for r in groups:
        merged = ya_ref[r, :].astype(F32) + sgb_ref[r, :].astype(F32) * yb_ref[r, :].astype(F32)
        outs.append(_dot(merged.astype(BF16), wo_ref[...]))
    x1s = [x_ref[r, :] + _rms(o, gpm_ref[...]) for r, o in zip(groups, outs)]
    h2s = [_rms(x1, gpf_ref[...]).astype(BF16) for x1 in x1s]
    a_next = [up(h2, 0) for h2 in h2s]
    accs = [jnp.zeros(x1.shape, F32) for x1 in x1s]
    for c in range(n_chunks):
        for i, h2 in enumerate(h2s):
            a = jnp.square(jnp.maximum(a_next[i], 0.0)).astype(BF16)
            if c + 1 < n_chunks:
                a_next[i] = up(h2, c + 1)
            accs[i] = accs[i] + _dot(a, w2_ref[c * ff_chunk:(c + 1) * ff_chunk, :])
    for r, x1, acc in zip(groups, x1s, accs):
        o_ref[r, :] = x1 + _rms(acc, gqf_ref[...])


def _post_call(x2d, ya, sgb, yb, wts, l, *, tm):
    rows, d = x2d.shape
    wo, w1, w2, gpm, gpf, gqf = wts
    d_ff = w1.shape[-1]
    row_map = lambda i: (i, 0)
    lay_map = lambda i: (l, 0, 0)

    def wspec(a):
        return pl.BlockSpec((None,) + a.shape[1:], lay_map, pipeline_mode=pl.Buffered(1))

    rspec = pl.BlockSpec((tm, d), row_map)
    return pl.pallas_call(
        functools.partial(_post_kernel, d_ff=d_ff, ff_chunk=min(d_ff, 1024),
                          n_split=POST_ROW_GROUPS if tm % (POST_ROW_GROUPS * 2 * SUBLANE) == 0 else 1),
        out_shape=jax.ShapeDtypeStruct((rows, d), F32),
        grid=(rows // tm,),
        in_specs=[rspec, rspec, rspec, rspec, wspec(wo), wspec(w1), wspec(w2),
                  wspec(gpm), wspec(gpf), wspec(gqf)],
        out_specs=rspec,
        compiler_params=pltpu.CompilerParams(
            dimension_semantics=("arbitrary",),
            vmem_limit_bytes=VMEM_LIMIT_BYTES),
        name="post",
    )(x2d, ya, sgb, yb, wo, w1, w2, gpm, gpf, gqf)


def _rope_tables(pos, rope):
    inv = ROPE_THETA ** (-jnp.arange(0, rope, 2, dtype=F32) / rope)
    ang = pos.astype(F32)[:, None] * inv[None, :]
    cos, sin = jnp.cos(ang), jnp.sin(ang)
    z = jnp.zeros_like(cos)
    pad = jnp.zeros((pos.shape[0], LANE - rope), F32)
    cos_t = jnp.concatenate([cos, cos, pad], axis=1)
    sin_a = jnp.concatenate([z, sin, pad], axis=1)
    sin_b = jnp.concatenate([-sin, z, pad], axis=1)
    return cos_t, sin_a, sin_b


def _prep_weights(w_in, conv_w, w_uq, w_uk, w_uv, w_o, w1, w2, g_pre_mix, g_post_mix, g_pre_ffn,
                  g_post_ffn, g_q, g_kv, *, nope, rope):
    depth, d, _ = w_in.shape
    q_lora = w_uq.shape[1]
    off_ga = w_in.shape[-1] - 2 * d
    w_a = jnp.pad(w_in[:, :, d:off_ga],
                  ((0, 0), (0, 0), (0, LANE - rope))).astype(BF16)
    w_b = w_in[:, :, :d].astype(BF16)
    w_g = w_in[:, :, off_ga:].astype(BF16)
    uq = w_uq.reshape(depth, q_lora, N_HEADS, nope + rope)
    wuq = jnp.concatenate(
        [uq[..., :nope].reshape(depth, q_lora, N_HEADS * nope),
         uq[..., nope:].reshape(depth, q_lora, N_HEADS * rope)], axis=2).astype(BF16)
    wuk, wuv = w_uk.astype(BF16), w_uv.astype(BF16)
    g3 = lambda g: g[:, None, :]
    proj_w = (w_a, w_b, w_g, wuq, wuk, wuv, conv_w, g3(g_pre_mix), g3(g_q), g3(g_kv))
    post_w = (w_o.astype(BF16), w1.astype(BF16), w2.astype(BF16),
              g3(g_post_mix), g3(g_pre_ffn), g3(g_post_ffn))
    return proj_w, post_w, wuk, wuv


def _tile(n, pref):
    t = min(n, pref)
    assert n % t == 0, (n, t)
    return t


def kernel(x_prompt, x_sample, cache_ckv, cache_kpe, state_conv, w_in, conv_w, w_uq, w_uk, w_uv,
           w_o, w1, w2, g_pre_mix, g_post_mix, g_pre_ffn, g_post_ffn, g_q, g_kv):
    b, s, d = x_prompt.shape
    bs, ls, _ = x_sample.shape
    depth, _, p, kv_lora = cache_ckv.shape
    rope = cache_kpe.shape[-1]
    q_lora = g_q.shape[-1]
    nope = w_uk.shape[-1] // N_HEADS
    v_dim = w_uv.shape[-1] // N_HEADS
    assert nope == LANE and v_dim == LANE and rope * 2 == LANE and d == N_HEADS * v_dim
    assert conv_w.shape[1] == CONV_WIDTH and s % CHUNK == 0 and ls >= CONV_WIDTH - 1
    q_scale = float(nope + rope) ** -0.5 * float(np.log2(np.e))
    dims = (d, q_lora, kv_lora, rope, q_scale)

    assert q_lora % LANE == 0 and kv_lora % LANE == 0
    proj_w, post_w, wuk, wuv = _prep_weights(
        w_in, conv_w, w_uq, w_uk, w_uv, w_o, w1, w2, g_pre_mix, g_post_mix, g_pre_ffn,
        g_post_ffn, g_q, g_kv, nope=nope, rope=rope)
    cache_kpe_t = jnp.swapaxes(cache_kpe, 2, 3)

    tm_p = _tile(s, 512)
    tq = _tile(s, 512)
    segs_per_tile = max(1, min(bs, 512 // ls))
    assert bs % segs_per_tile == 0
    tm_s = segs_per_tile * ls
    tabs_p = _rope_tables(jnp.arange(s, dtype=jnp.int32), rope)
    tabs_s = tuple(jnp.tile(t, (segs_per_tile, 1))
                   for t in _rope_tables(p + jnp.arange(ls, dtype=jnp.int32), rope))
    pos_new = p + np.arange(ls)
    new_all_visible = bool(np.all((pos_new[None, :] // CHUNK) <= (pos_new[:, None] // CHUNK)))
    new_mask = None if new_all_visible else p

    xp = x_prompt.reshape(b * s, d)
    xs = x_sample.reshape(bs * ls, d)
    zero_buf = jnp.zeros((b, CONV_WIDTH - 1, d), F32)
    stk_p, stk_s = (), ()
    for l in range(depth):
        ya, sgb, q, *stk_p, k, v = _proj_call(
            xp, zero_buf, tabs_p, proj_w, l, stk_p, n_outer=b, n_inner=s // tm_p, tm=tm_p,
            seg=tm_p, dims=dims, expand_kv=True)
        yb = _attn_prompt_call(q.reshape(b, s, -1), k.reshape(b, s, -1), v.reshape(b, s, -1),
                               b=b, s=s, tq=tq)
        xp = _post_call(xp, ya, sgb, yb.reshape(b * s, d), post_w, l, tm=tm_p)

        ya, sgb, q, *stk_s = _proj_call(
            xs, state_conv[l], tabs_s, proj_w, l, stk_s, n_outer=bs // segs_per_tile, n_inner=1,
            tm=tm_s, seg=ls, dims=dims, expand_kv=False)
        yb = _attn_sample_call(q.reshape(bs, ls, -1), cache_ckv, cache_kpe_t,
                               stk_s[0].reshape(depth, bs, ls, kv_lora),
                               stk_s[1].reshape(depth, bs, ls, rope),
                               wuk, wuv, l, b=bs, ls=ls, new_mask=new_mask)
        xs = _post_call(xs, ya, sgb, yb.reshape(bs * ls, d), post_w, l, tm=tm_s)

    return (xp.reshape(b, s, d), xs.reshape(bs, ls, d),
            stk_p[0].reshape(depth, b, s, kv_lora), stk_p[1].reshape(depth, b, s, rope), stk_p[2],
            stk_s[0].reshape(depth, bs, ls, kv_lora), stk_s[1].reshape(depth, bs, ls, rope),
            stk_s[2])
```

```python
import functools

import numpy as np
import jax
import jax.numpy as jnp
from jax import lax
from jax.experimental import pallas as pl
from jax.experimental.pallas import tpu as pltpu

CHUNK = 64
CONV_WIDTH = 3
N_HEADS = 8
ROPE_THETA = 10000.0
EPS = 1e-6
NEG_INF = -1e30

LANE = 128
SUBLANE = 8
HEAD_PAD = 2 * LANE
VMEM_LIMIT_BYTES = 56 * 1024 * 1024
ROW_TILE = 512
POST_ROW_GROUPS = 2
PROJ_ROW_GROUPS = 2
SAMPLE_BATCHES_PER_STEP = 2

F32 = jnp.float32
BF16 = jnp.bfloat16


def _rms(x, g):
    ms = jnp.mean(x * x, axis=-1, keepdims=True)
    return x * lax.rsqrt(ms + EPS) * g


def _rope(pe, cos_t, sin_a, sin_b):
    return (pe * cos_t + pltpu.roll(pe, LANE // 4, 1) * sin_a
            + pltpu.roll(pe, LANE - LANE // 4, 1) * sin_b)


def _dot(a, b):
    return jnp.dot(a, b, preferred_element_type=F32)


def _dot_nt(a, b):
    return lax.dot_general(a, b, (((1,), (1,)), ((), ())), preferred_element_type=F32)


def _proj_kernel(x_ref, init_ref, cos_ref, sina_ref, sinb_ref,
                 wa_ref, wb_ref, wg_ref, wuq_ref, wuk_ref, wuv_ref, convw_ref,
                 gpre_ref, gq_ref, gkv_ref, *rest,
                 d, seg, n_groups, q_lora, kv_lora, rope, sm_scale, expand_kv, n_prev):
    ya_ref, sgb_ref, q_ref, ckv_ref, kpe_ref, cbuf_ref, *rest = rest[n_prev:]
    if expand_kv:
        k_ref, v_ref, carry_ref = rest
    else:
        (carry_ref,) = rest
    inner = pl.program_id(1)
    gm = x_ref.shape[0] // n_groups
    groups = [pl.ds(g * gm, gm) for g in range(n_groups)]
    off_q = 2 * d
    off_kv = q_lora
    off_pe = off_kv + kv_lora

    @pl.when(inner == 0)
    def _():
        carry_ref[...] = init_ref[...]

    hs = [_rms(x_ref[r, :], gpre_ref[...]).astype(BF16) for r in groups]
    tabs = [(cos_ref[r, :], sina_ref[r, :], sinb_ref[r, :]) for r in groups]
    lats = [_dot(h, wa_ref[:, off_q:]) for h in hs]
    zcs = [_dot(h, wa_ref[:, :d]) for h in hs]

    for r, lat, tab in zip(groups, lats, tabs):
        cqn = _rms(lat[:, :off_kv], gq_ref[...]).astype(BF16)
        qn = _dot(cqn, wuq_ref[:, :N_HEADS * LANE]) * sm_scale
        qp = _dot(cqn, wuq_ref[:, N_HEADS * LANE:]) * sm_scale
        for hh in range(N_HEADS):
            pe = qp[:, (hh // 2) * LANE:(hh // 2 + 1) * LANE]
            if hh % 2:
                pe = pltpu.roll(pe, LANE // 2, 1)
            q_ref[r, hh * HEAD_PAD:hh * HEAD_PAD + LANE] = qn[:, hh * LANE:(hh + 1) * LANE].astype(BF16)
            q_ref[r, hh * HEAD_PAD + LANE:(hh + 1) * HEAD_PAD] = _rope(pe, *tab).astype(BF16)

    for r, lat, tab in zip(groups, lats, tabs):
        ckv = _rms(lat[:, off_kv:off_pe], gkv_ref[...])
        ckv_ref[r, :] = ckv
        kpe = _rope(lat[:, off_pe:], *tab)
        kpe_ref[r, :] = kpe[:, :rope]
        if expand_kv:
            ckvn = ckv.astype(BF16)
            kpe_b = kpe.astype(BF16)
            for hp in range(N_HEADS // 2):
                kn = _dot(ckvn, wuk_ref[:, hp * HEAD_PAD:(hp + 1) * HEAD_PAD]).astype(BF16)
                base = 2 * hp * HEAD_PAD
                k_ref[r, base:base + LANE] = kn[:, :LANE]
                k_ref[r, base + LANE:base + HEAD_PAD] = kpe_b
                k_ref[r, base + HEAD_PAD:base + HEAD_PAD + LANE] = kn[:, LANE:]
                k_ref[r, base + HEAD_PAD + LANE:base + 2 * HEAD_PAD] = kpe_b
            v_ref[r, :] = _dot(ckvn, wuv_ref[...]).astype(BF16)

    for r, h in zip(groups, hs):
        sgb_ref[r, :] = jax.nn.sigmoid(_dot(h, wg_ref[:, d:])).astype(BF16)

    us = [zc * _dot(h, wa_ref[:, d:off_q]) for zc, h in zip(zcs, hs)]
    w0 = convw_ref[0:1, :]
    w1 = convw_ref[1:2, :]
    w2 = convw_ref[2:3, :]
    rows = lax.broadcasted_iota(jnp.int32, (SUBLANE, 1), 0)
    pm = min(seg, gm)
    ycs, tail = [], None
    for g, u in enumerate(us):
        parts = []
        for p in range(gm // pm):
            row0 = g * gm + p * pm
            sidx = row0 // seg
            up = u[p * pm:(p + 1) * pm]
            init = carry_ref[sidx] if row0 % seg == 0 else tail
            i0, i1 = init[0:1], init[1:2]
            r1, r2 = pltpu.roll(up, 1, 0), pltpu.roll(up, 2, 0)
            h1 = jnp.where(rows == 0, i1, r1[:SUBLANE])
            h2 = jnp.where(rows == 0, i0, jnp.where(rows == 1, i1, r2[:SUBLANE]))
            u1 = jnp.concatenate([h1, r1[SUBLANE:]], axis=0)
            u2 = jnp.concatenate([h2, r2[SUBLANE:]], axis=0)
            parts.append(w0 * u2 + w1 * u1 + w2 * up)
            tail = up[pm - 2:pm]
            if (row0 + pm) % seg == 0:
                carry_ref[sidx] = tail
                cbuf_ref[sidx] = tail
        ycs.append(parts[0] if len(parts) == 1 else jnp.concatenate(parts, axis=0))

    gas = [_dot(h, wg_ref[:, :d]) for h in hs]
    zbs = [_dot(h, wb_ref[...]) for h in hs]
    for r, ga, zb, yc in zip(groups, gas, zbs, ycs):
        ya_ref[r, :] = (jax.nn.sigmoid(ga) * zb * yc).astype(BF16)


def _proj_call(x2d, conv_init, tabs, wts, l, prev_stacks, *, n_outer, n_inner, tm, seg, dims,
               expand_kv):
    d, q_lora, kv_lora, rope, sm_scale = dims
    depth = wts[0].shape[0]
    nseg = tm // seg
    rows = x2d.shape[0]
    row_map = lambda o, i: (o * n_inner + i, 0)
    tab_map = lambda o, i: (i, 0)
    seg_map = lambda o, i: (o, 0, 0)
    lay_map = lambda o, i: (l, 0, 0)
    lrow_map = lambda o, i: (l, o * n_inner + i, 0)

    def wspec(a):
        return pl.BlockSpec((None,) + a.shape[1:], lay_map, pipeline_mode=pl.Buffered(1))

    def rspec(n):
        return pl.BlockSpec((tm, n), row_map)

    gm = tm // PROJ_ROW_GROUPS
    n_groups = PROJ_ROW_GROUPS if gm % (2 * SUBLANE) == 0 and (gm % seg == 0 or seg % gm == 0) else 1
    kern = functools.partial(_proj_kernel, d=d, seg=seg, n_groups=n_groups, q_lora=q_lora,
                             kv_lora=kv_lora, rope=rope, sm_scale=sm_scale,
                             expand_kv=expand_kv, n_prev=len(prev_stacks))
    out_shape = [
        jax.ShapeDtypeStruct((rows, d), BF16),
        jax.ShapeDtypeStruct((rows, d), BF16),
        jax.ShapeDtypeStruct((rows, N_HEADS * HEAD_PAD), BF16),
        jax.ShapeDtypeStruct((depth, rows, kv_lora), F32),
        jax.ShapeDtypeStruct((depth, rows, rope), F32),
        jax.ShapeDtypeStruct((depth,) + conv_init.shape, F32),
    ]
    out_specs = [rspec(d), rspec(d), rspec(N_HEADS * HEAD_PAD),
                 pl.BlockSpec((None, tm, kv_lora), lrow_map),
                 pl.BlockSpec((None, tm, rope), lrow_map),
                 pl.BlockSpec((None, nseg, CONV_WIDTH - 1, d), lambda o, i: (l, o, 0, 0))]
    if expand_kv:
        out_shape += [jax.ShapeDtypeStruct((rows, N_HEADS * HEAD_PAD), BF16),
                      jax.ShapeDtypeStruct((rows, N_HEADS * LANE), BF16)]
        out_specs += [rspec(N_HEADS * HEAD_PAD), rspec(N_HEADS * LANE)]
    in_specs = [rspec(d), pl.BlockSpec((nseg, CONV_WIDTH - 1, d), seg_map),
                pl.BlockSpec((tm, LANE), tab_map), pl.BlockSpec((tm, LANE), tab_map),
                pl.BlockSpec((tm, LANE), tab_map),
                *[wspec(w) for w in wts]]
    n_in = len(in_specs)
    in_specs += [pl.BlockSpec(memory_space=pl.ANY)] * len(prev_stacks)
    stack_out0 = 3
    return pl.pallas_call(
        kern,
        out_shape=out_shape,
        grid=(n_outer, n_inner),
        in_specs=in_specs,
        out_specs=out_specs,
        scratch_shapes=[pltpu.VMEM((nseg, CONV_WIDTH - 1, d), F32)],
        input_output_aliases={n_in + j: stack_out0 + j for j in range(len(prev_stacks))},
        compiler_params=pltpu.CompilerParams(
            dimension_semantics=("arbitrary", "arbitrary"),
            vmem_limit_bytes=VMEM_LIMIT_BYTES),
        name="proj",
    )(x2d, conv_init, *tabs, *wts, *prev_stacks)


def _attn_prompt_kernel(q_ref, k_ref, v_ref, o_ref, *, tq, nq):
    i = pl.program_id(1)
    ncol = tq // LANE
    r_chunk = lax.broadcasted_iota(jnp.int32, (tq, LANE), 0) // CHUNK
    ones = jnp.ones((tq, LANE), BF16)

    def head_block(j, hh, masked, state):
        koff = j * tq
        q = q_ref[:, hh * HEAD_PAD:(hh + 1) * HEAD_PAD]
        k = k_ref[koff:koff + tq, hh * HEAD_PAD:(hh + 1) * HEAD_PAD]
        v = v_ref[koff:koff + tq, hh * LANE:(hh + 1) * LANE]
        s = _dot_nt(q, k)
        cols = []
        for c in range(ncol):
            sc = s[:, c * LANE:(c + 1) * LANE]
            if masked:
                c_chunk = (lax.broadcasted_iota(jnp.int32, (tq, LANE), 1) + c * LANE) // CHUNK
                sc = jnp.where(c_chunk <= r_chunk, sc, NEG_INF)
            cols.append(sc)
        m_cur = jnp.max(functools.reduce(jnp.maximum, cols), axis=1, keepdims=True)
        m_next = (jnp.broadcast_to(m_cur, (tq, LANE)) if state is None
                  else jnp.maximum(state[0], m_cur))
        p = jnp.concatenate([jnp.exp2(sc - m_next) for sc in cols], axis=1).astype(BF16)
        pv = _dot(p, jnp.concatenate([v, ones], axis=1))
        if state is None:
            return m_next, pv
        alpha = jnp.exp2(state[0] - m_next)
        return m_next, jnp.concatenate([alpha, alpha], axis=1) * state[1] + pv

    for qi in range(nq):
        @pl.when(i == qi)
        def _(qi=qi):
            state = [None] * N_HEADS
            for j in range(qi + 1):
                for hh in range(N_HEADS):
                    state[hh] = head_block(j, hh, j == qi, state[hh])
            for hh in range(N_HEADS):
                acc = state[hh][1]
                o_ref[:, hh * LANE:(hh + 1) * LANE] = (acc[:, :LANE] / acc[:, LANE:]).astype(BF16)


def _attn_prompt_call(q, k, v, *, b, s, tq):
    nq = s // tq
    return pl.pallas_call(
        functools.partial(_attn_prompt_kernel, tq=tq, nq=nq),
        out_shape=jax.ShapeDtypeStruct((b, s, N_HEADS * LANE), BF16),
        grid=(b, nq),
        in_specs=[pl.BlockSpec((None, tq, N_HEADS * HEAD_PAD), lambda bb, i: (bb, i, 0)),
                  pl.BlockSpec((None, s, N_HEADS * HEAD_PAD), lambda bb, i: (bb, 0, 0)),
                  pl.BlockSpec((None, s, N_HEADS * LANE), lambda bb, i: (bb, 0, 0))],
        out_specs=pl.BlockSpec((None, tq, N_HEADS * LANE), lambda bb, i: (bb, i, 0)),
        compiler_params=pltpu.CompilerParams(
            dimension_semantics=("arbitrary", "arbitrary"),
            vmem_limit_bytes=VMEM_LIMIT_BYTES),
        name="attn_prompt",
    )(q, k, v)


def _attn_sample_kernel(q_ref, pckv_ref, pkpe_t_ref, nckv_ref, nkpe_ref, wuk_ref, wuv_ref,
                        o_ref, *, ls, rope, new_mask, nb):
    batches = range(nb)
    qas, qps = [], []
    for bi in batches:
        qa, qp = [], []
        for hh in range(N_HEADS):
            qn = q_ref[bi, :, hh * HEAD_PAD:hh * HEAD_PAD + LANE]
            qa.append(_dot_nt(qn, wuk_ref[:, hh * LANE:(hh + 1) * LANE]).astype(BF16))
            qp.append(q_ref[bi, :, hh * HEAD_PAD + LANE:hh * HEAD_PAD + LANE + rope])
        qas.append(jnp.concatenate(qa, axis=0))
        qps.append(jnp.concatenate(qp, axis=0))
    pckvs = [pckv_ref[bi].astype(BF16) for bi in batches]
    nckvs = [nckv_ref[bi].astype(BF16) for bi in batches]
    scores = []
    for bi in batches:
        s_past = _dot_nt(qas[bi], pckvs[bi]) + _dot(qps[bi], pkpe_t_ref[bi].astype(BF16))
        s_new = _dot_nt(qas[bi], nckvs[bi]) + _dot_nt(qps[bi], nkpe_ref[bi].astype(BF16))
        if new_mask is not None:
            r = lax.broadcasted_iota(jnp.int32, s_new.shape, 0) % ls
            c = lax.broadcasted_iota(jnp.int32, s_new.shape, 1)
            p0 = new_mask
            s_new = jnp.where((p0 + c) // CHUNK <= (p0 + r) // CHUNK, s_new, NEG_INF)
        scores.append((s_past, s_new))
    o_lats = []
    for bi, (s_past, s_new) in enumerate(scores):
        m = jnp.maximum(jnp.max(s_past, axis=1, keepdims=True),
                        jnp.max(s_new, axis=1, keepdims=True))
        p_past = jnp.exp2(s_past - m)
        p_new = jnp.exp2(s_new - m)
        denom = jnp.sum(p_past, axis=1, keepdims=True) + jnp.sum(p_new, axis=1, keepdims=True)
        o_lat = (_dot(p_past.astype(BF16), pckvs[bi]) + _dot(p_new.astype(BF16), nckvs[bi])) / denom
        o_lats.append(o_lat.astype(BF16))
    for bi, o_lat in enumerate(o_lats):
        for hh in range(N_HEADS):
            o_ref[bi, :, hh * LANE:(hh + 1) * LANE] = _dot(
                o_lat[hh * ls:(hh + 1) * ls], wuv_ref[:, hh * LANE:(hh + 1) * LANE]).astype(BF16)


def _attn_sample_call(q, cache_ckv, cache_kpe_t, nckv, nkpe, wuk, wuv, l, *, b, ls, new_mask):
    p, kv_lora = cache_ckv.shape[2], cache_ckv.shape[3]
    rope = cache_kpe_t.shape[2]
    nb = SAMPLE_BATCHES_PER_STEP if b % SAMPLE_BATCHES_PER_STEP == 0 else 1
    bmap = lambda bb: (bb, 0, 0)
    lmap = lambda bb: (l, bb, 0, 0)
    return pl.pallas_call(
        functools.partial(_attn_sample_kernel, ls=ls, rope=rope, new_mask=new_mask, nb=nb),
        out_shape=jax.ShapeDtypeStruct((b, ls, N_HEADS * LANE), BF16),
        grid=(b // nb,),
        in_specs=[pl.BlockSpec((nb, ls, N_HEADS * HEAD_PAD), bmap),
                  pl.BlockSpec((None, nb, p, kv_lora), lmap),
                  pl.BlockSpec((None, nb, rope, p), lmap),
                  pl.BlockSpec((None, nb, ls, kv_lora), lmap),
                  pl.BlockSpec((None, nb, ls, rope), lmap),
                  pl.BlockSpec((None,) + wuk.shape[1:], lambda bb: (l, 0, 0)),
                  pl.BlockSpec((None,) + wuv.shape[1:], lambda bb: (l, 0, 0))],
        out_specs=pl.BlockSpec((nb, ls, N_HEADS * LANE), bmap),
        compiler_params=pltpu.CompilerParams(
            dimension_semantics=("arbitrary",),
            vmem_limit_bytes=VMEM_LIMIT_BYTES),
        name="attn_sample",
    )(q, cache_ckv, cache_kpe_t, nckv, nkpe, wuk, wuv)


def _post_kernel(x_ref, ya_ref, sgb_ref, yb_ref, wo_ref, w1_ref, w2_ref,
                 gpm_ref, gpf_ref, gqf_ref, o_ref, *, d_ff, ff_chunk, n_split):
    tm = x_ref.shape[0]
    groups = [pl.ds(i * (tm // n_split), tm // n_split) for i in range(n_split)]
    n_chunks = d_ff // ff_chunk
    up = lambda h2, c: _dot(h2, w1_ref[:, c * ff_chunk:(c + 1) * ff_chunk])

    outs = []
    for r in groups:
        merged = ya_ref[r, :].astype(F32) + sgb_ref[r, :].astype(F32) * yb_ref[r, :].astype(F32)
        outs.append(_dot(merged.astype(BF16), wo_ref[...]))
    x1s = [x_ref[r, :] + _rms(o, gpm_ref[...]) for r, o in zip(groups, outs)]
    h2s = [_rms(x1, gpf_ref[...]).astype(BF16) for x1 in x1s]
    a_next = [up(h2, 0) for h2 in h2s]
    accs = [jnp.zeros(x1.shape, F32) for x1 in x1s]
    for c in range(n_chunks):
        for i, h2 in enumerate(h2s):
            a = jnp.square(jnp.maximum(a_next[i], 0.0)).astype(BF16)
            if c + 1 < n_chunks:
                a_next[i] = up(h2, c + 1)
            accs[i] = accs[i] + _dot(a, w2_ref[c * ff_chunk:(c + 1) * ff_chunk, :])
    for r, x1, acc in zip(groups, x1s, accs):
        o_ref[r, :] = x1 + _rms(acc, gqf_ref[...])


def _post_call(x2d, ya, sgb, yb, wts, l, *, tm):
    rows, d = x2d.shape
    wo, w1, w2, gpm, gpf, gqf = wts
    d_ff = w1.shape[-1]
    row_map = lambda i: (i, 0)
    lay_map = lambda i: (l, 0, 0)

    def wspec(a):
        return pl.BlockSpec((None,) + a.shape[1:], lay_map, pipeline_mode=pl.Buffered(1))

    rspec = pl.BlockSpec((tm, d), row_map)
    return pl.pallas_call(
        functools.partial(_post_kernel, d_ff=d_ff, ff_chunk=min(d_ff, 1024),
                          n_split=POST_ROW_GROUPS if tm % (POST_ROW_GROUPS * 2 * SUBLANE) == 0 else 1),
        out_shape=jax.ShapeDtypeStruct((rows, d), F32),
        grid=(rows // tm,),
        in_specs=[rspec, rspec, rspec, rspec, wspec(wo), wspec(w1), wspec(w2),
                  wspec(gpm), wspec(gpf), wspec(gqf)],
        out_specs=rspec,
        compiler_params=pltpu.CompilerParams(
            dimension_semantics=("arbitrary",),
            vmem_limit_bytes=VMEM_LIMIT_BYTES),
        name="post",
    )(x2d, ya, sgb, yb, wo, w1, w2, gpm, gpf, gqf)


def _rope_tables(pos, rope):
    inv = ROPE_THETA ** (-jnp.arange(0, rope, 2, dtype=F32) / rope)
    ang = pos.astype(F32)[:, None] * inv[None, :]
    cos, sin = jnp.cos(ang), jnp.sin(ang)
    z = jnp.zeros_like(cos)
    pad = jnp.zeros((pos.shape[0], LANE - rope), F32)
    cos_t = jnp.concatenate([cos, cos, pad], axis=1)
    sin_a = jnp.concatenate([z, sin, pad], axis=1)
    sin_b = jnp.concatenate([-sin, z, pad], axis=1)
    return cos_t, sin_a, sin_b


def _prep_weights(w_in, conv_w, w_uq, w_uk, w_uv, w_o, w1, w2, g_pre_mix, g_post_mix, g_pre_ffn,
                  g_post_ffn, g_q, g_kv, *, nope, rope):
    depth, d, _ = w_in.shape
    q_lora = w_uq.shape[1]
    off_ga = w_in.shape[-1] - 2 * d
    w_a = jnp.pad(w_in[:, :, d:off_ga],
                  ((0, 0), (0, 0), (0, LANE - rope))).astype(BF16)
    w_b = w_in[:, :, :d].astype(BF16)
    w_g = w_in[:, :, off_ga:].astype(BF16)
    uq = w_uq.reshape(depth, q_lora, N_HEADS, nope + rope)
    wuq = jnp.concatenate(
        [uq[..., :nope].reshape(depth, q_lora, N_HEADS * nope),
         uq[..., nope:].reshape(depth, q_lora, N_HEADS * rope)], axis=2).astype(BF16)
    wuk, wuv = w_uk.astype(BF16), w_uv.astype(BF16)
    g3 = lambda g: g[:, None, :]
    proj_w = (w_a, w_b, w_g, wuq, wuk, wuv, conv_w, g3(g_pre_mix), g3(g_q), g3(g_kv))
    post_w = (w_o.astype(BF16), w1.astype(BF16), w2.astype(BF16),
              g3(g_post_mix), g3(g_pre_ffn), g3(g_post_ffn))
    return proj_w, post_w, wuk, wuv


def _tile(n, pref):
    t = min(n, pref)
    assert n % t == 0, (n, t)
    return t


def kernel(x_prompt, x_sample, cache_ckv, cache_kpe, state_conv, w_in, conv_w, w_uq, w_uk, w_uv,
           w_o, w1, w2, g_pre_mix, g_post_mix, g_pre_ffn, g_post_ffn, g_q, g_kv):
    b, s, d = x_prompt.shape
    bs, ls, _ = x_sample.shape
    depth, _, p, kv_lora = cache_ckv.shape
    rope = cache_kpe.shape[-1]
    q_lora = g_q.shape[-1]
    nope = w_uk.shape[-1] // N_HEADS
    v_dim = w_uv.shape[-1] // N_HEADS
    assert nope == LANE and v_dim == LANE and rope * 2 == LANE and d == N_HEADS * v_dim
    assert conv_w.shape[1] == CONV_WIDTH and s % CHUNK == 0 and ls >= CONV_WIDTH - 1
    q_scale = float(nope + rope) ** -0.5 * float(np.log2(np.e))
    dims = (d, q_lora, kv_lora, rope, q_scale)

    assert q_lora % LANE == 0 and kv_lora % LANE == 0
    proj_w, post_w, wuk, wuv = _prep_weights(
        w_in, conv_w, w_uq, w_uk, w_uv, w_o, w1, w2, g_pre_mix, g_post_mix, g_pre_ffn,
        g_post_ffn, g_q, g_kv, nope=nope, rope=rope)
    cache_kpe_t = jnp.swapaxes(cache_kpe, 2, 3)

    tm_p = _tile(s, ROW_TILE)
    tq = _tile(s, ROW_TILE)
    segs_per_tile = max(1, min(bs, ROW_TILE // ls))
    assert bs % segs_per_tile == 0
    tm_s = segs_per_tile * ls
    tabs_p = _rope_tables(jnp.arange(s, dtype=jnp.int32), rope)
    tabs_s = tuple(jnp.tile(t, (segs_per_tile, 1))
                   for t in _rope_tables(p + jnp.arange(ls, dtype=jnp.int32), rope))
    pos_new = p + np.arange(ls)
    new_all_visible = bool(np.all((pos_new[None, :] // CHUNK) <= (pos_new[:, None] // CHUNK)))
    new_mask = None if new_all_visible else p

    xp = x_prompt.reshape(b * s, d)
    xs = x_sample.reshape(bs * ls, d)
    zero_buf = jnp.zeros((b, CONV_WIDTH - 1, d), F32)
    stk_p, stk_s = (), ()
    for l in range(depth):
        ya, sgb, q, *stk_p, k, v = _proj_call(
            xp, zero_buf, tabs_p, proj_w, l, stk_p, n_outer=b, n_inner=s // tm_p, tm=tm_p,
            seg=tm_p, dims=dims, expand_kv=True)
        yb = _attn_prompt_call(q.reshape(b, s, -1), k.reshape(b, s, -1), v.reshape(b, s, -1),
                               b=b, s=s, tq=tq)
        xp = _post_call(xp, ya, sgb, yb.reshape(b * s, d), post_w, l, tm=tm_p)

        ya, sgb, q, *stk_s = _proj_call(
            xs, state_conv[l], tabs_s, proj_w, l, stk_s, n_outer=bs // segs_per_tile, n_inner=1,
            tm=tm_s, seg=ls, dims=dims, expand_kv=False)
        yb = _attn_sample_call(q.reshape(bs, ls, -1), cache_ckv, cache_kpe_t,
                               stk_s[0].reshape(depth, bs, ls, kv_lora),
                               stk_s[1].reshape(depth, bs, ls, rope),
                               wuk, wuv, l, b=bs, ls=ls, new_mask=new_mask)
        xs = _post_call(xs, ya, sgb, yb.reshape(bs * ls, d), post_w, l, tm=tm_s)

    return (xp.reshape(b, s, d), xs.reshape(bs, ls, d),
            stk_p[0].reshape(depth, b, s, kv_lora), stk_p[1].reshape(depth, b, s, rope), stk_p[2],
            stk_s[0].reshape(depth, bs, ls, kv_lora), stk_s[1].reshape(depth, bs, ls, rope),
            stk_s[2])
```

```python
import functools

import numpy as np
import jax
import jax.numpy as jnp
from jax import lax
from jax.experimental import pallas as pl
from jax.experimental.pallas import tpu as pltpu

CHUNK = 64
CONV_WIDTH = 3
N_HEADS = 8
ROPE_THETA = 10000.0
EPS = 1e-6
NEG_INF = -1e30

LANE = 128
SUBLANE = 8
HEAD_PAD = 2 * LANE
VMEM_LIMIT_BYTES = 56 * 1024 * 1024
ROW_TILE = 512
POST_ROW_GROUPS = 2
PROJ_ROW_GROUPS = 2
SAMPLE_BATCHES_PER_STEP = 2

F32 = jnp.float32
BF16 = jnp.bfloat16


def _rms(x, g):
    ms = jnp.mean(x * x, axis=-1, keepdims=True)
    return x * lax.rsqrt(ms + EPS) * g


def _rope(pe, cos_t, sin_a, sin_b):
    return (pe * cos_t + pltpu.roll(pe, LANE // 4, 1) * sin_a
            + pltpu.roll(pe, LANE - LANE // 4, 1) * sin_b)


def _dot(a, b):
    return jnp.dot(a, b, preferred_element_type=F32)


def _dot_nt(a, b):
    return lax.dot_general(a, b, (((1,), (1,)), ((), ())), preferred_element_type=F32)


def _proj_kernel(x_ref, init_ref, tab_ref,
                 wa_ref, wb_ref, wg_ref, wuq_ref, wuk_ref, wuv_ref, convw_ref,
                 gpre_ref, gq_ref, gkv_ref, *rest,
                 d, seg, n_groups, q_lora, kv_lora, rope, sm_scale, expand_kv, n_prev):
    yg_ref, qkv_ref, ckv_ref, kpe_ref, cbuf_ref, carry_ref = rest[n_prev:]
    off_k = N_HEADS * HEAD_PAD
    off_v = 2 * off_k
    inner = pl.program_id(1)
    gm = x_ref.shape[0] // n_groups
    groups = [pl.ds(g * gm, gm) for g in range(n_groups)]
    off_q = 2 * d
    off_kv = q_lora
    off_pe = off_kv + kv_lora

    @pl.when(inner == 0)
    def _():
        carry_ref[...] = init_ref[...]

    hs = [_rms(x_ref[r, :], gpre_ref[...]).astype(BF16) for r in groups]
    tabs = [tuple(tab_ref[r, t * LANE:(t + 1) * LANE] for t in range(3)) for r in groups]
    lats = [_dot(h, wa_ref[:, off_q:]) for h in hs]
    zcs = [_dot(h, wa_ref[:, :d]) for h in hs]

    for r, lat, tab in zip(groups, lats, tabs):
        cqn = _rms(lat[:, :off_kv], gq_ref[...]).astype(BF16)
        qn = _dot(cqn, wuq_ref[:, :N_HEADS * LANE]) * sm_scale
        qp = _dot(cqn, wuq_ref[:, N_HEADS * LANE:]) * sm_scale
        for hh in range(N_HEADS):
            pe = qp[:, (hh // 2) * LANE:(hh // 2 + 1) * LANE]
            if hh % 2:
                pe = pltpu.roll(pe, LANE // 2, 1)
            qkv_ref[r, hh * HEAD_PAD:hh * HEAD_PAD + LANE] = qn[:, hh * LANE:(hh + 1) * LANE].astype(BF16)
            qkv_ref[r, hh * HEAD_PAD + LANE:(hh + 1) * HEAD_PAD] = _rope(pe, *tab).astype(BF16)

    for r, lat, tab in zip(groups, lats, tabs):
        ckv = _rms(lat[:, off_kv:off_pe], gkv_ref[...])
        ckv_ref[r, :] = ckv
        kpe = _rope(lat[:, off_pe:], *tab)
        kpe_ref[r, :] = kpe[:, :rope]
        if expand_kv:
            ckvn = ckv.astype(BF16)
            kpe_b = kpe.astype(BF16)
            for hp in range(N_HEADS // 2):
                kn = _dot(ckvn, wuk_ref[:, hp * HEAD_PAD:(hp + 1) * HEAD_PAD]).astype(BF16)
                base = off_k + 2 * hp * HEAD_PAD
                qkv_ref[r, base:base + LANE] = kn[:, :LANE]
                qkv_ref[r, base + LANE:base + HEAD_PAD] = kpe_b
                qkv_ref[r, base + HEAD_PAD:base + HEAD_PAD + LANE] = kn[:, LANE:]
                qkv_ref[r, base + HEAD_PAD + LANE:base + 2 * HEAD_PAD] = kpe_b
            qkv_ref[r, off_v:] = _dot(ckvn, wuv_ref[...]).astype(BF16)

    for r, h in zip(groups, hs):
        yg_ref[r, d:] = jax.nn.sigmoid(_dot(h, wg_ref[:, d:])).astype(BF16)

    us = [zc * _dot(h, wa_ref[:, d:off_q]) for zc, h in zip(zcs, hs)]
    w0 = convw_ref[0:1, :]
    w1 = convw_ref[1:2, :]
    w2 = convw_ref[2:3, :]
    rows = lax.broadcasted_iota(jnp.int32, (SUBLANE, 1), 0)
    pm = min(seg, gm)
    ycs, tail = [], None
    for g, u in enumerate(us):
        parts = []
        for p in range(gm // pm):
            row0 = g * gm + p * pm
            sidx = row0 // seg
            up = u[p * pm:(p + 1) * pm]
            init = carry_ref[sidx] if row0 % seg == 0 else tail
            i0, i1 = init[0:1], init[1:2]
            r1, r2 = pltpu.roll(up, 1, 0), pltpu.roll(up, 2, 0)
            h1 = jnp.where(rows == 0, i1, r1[:SUBLANE])
            h2 = jnp.where(rows == 0, i0, jnp.where(rows == 1, i1, r2[:SUBLANE]))
            u1 = jnp.concatenate([h1, r1[SUBLANE:]], axis=0)
            u2 = jnp.concatenate([h2, r2[SUBLANE:]], axis=0)
            parts.append(w0 * u2 + w1 * u1 + w2 * up)
            tail = up[pm - (CONV_WIDTH - 1):pm]
            if (row0 + pm) % seg == 0:
                carry_ref[sidx] = tail
                cbuf_ref[sidx] = tail
        ycs.append(parts[0] if len(parts) == 1 else jnp.concatenate(parts, axis=0))

    gas = [_dot(h, wg_ref[:, :d]) for h in hs]
    zbs = [_dot(h, wb_ref[...]) for h in hs]
    for r, ga, zb, yc in zip(groups, gas, zbs, ycs):
        yg_ref[r, :d] = (jax.nn.sigmoid(ga) * zb * yc).astype(BF16)


def _proj_call(x2d, conv_init, tabs, wts, l, prev_stacks, *, n_outer, n_inner, tm, seg, dims,
               expand_kv):
    d, q_lora, kv_lora, rope, sm_scale = dims
    depth = wts[0].shape[0]
    nseg = tm // seg
    rows = x2d.shape[0]
    row_map = lambda o, i: (o * n_inner + i, 0)
    tab_map = lambda o, i: (i, 0)
    seg_map = lambda o, i: (o, 0, 0)
    lay_map = lambda o, i: (l, 0, 0)
    lrow_map = lambda o, i: (l, o * n_inner + i, 0)

    def wspec(a):
        return pl.BlockSpec((None,) + a.shape[1:], lay_map, pipeline_mode=pl.Buffered(1))

    def rspec(n):
        return pl.BlockSpec((tm, n), row_map)

    gm = tm // PROJ_ROW_GROUPS
    n_groups = PROJ_ROW_GROUPS if gm % (2 * SUBLANE) == 0 and (gm % seg == 0 or seg % gm == 0) else 1
    kern = functools.partial(_proj_kernel, d=d, seg=seg, n_groups=n_groups, q_lora=q_lora,
                             kv_lora=kv_lora, rope=rope, sm_scale=sm_scale,
                             expand_kv=expand_kv, n_prev=len(prev_stacks))
    qkv_w = N_HEADS * (2 * HEAD_PAD + LANE) if expand_kv else N_HEADS * HEAD_PAD
    out_shape = [
        jax.ShapeDtypeStruct((rows, 2 * d), BF16),
        jax.ShapeDtypeStruct((rows, qkv_w), BF16),
        jax.ShapeDtypeStruct((depth, rows, kv_lora), F32),
        jax.ShapeDtypeStruct((depth, rows, rope), F32),
        jax.ShapeDtypeStruct((depth,) + conv_init.shape, F32),
    ]
    out_specs = [rspec(2 * d), rspec(qkv_w),
                 pl.BlockSpec((None, tm, kv_lora), lrow_map),
                 pl.BlockSpec((None, tm, rope), lrow_map),
                 pl.BlockSpec((None, nseg, CONV_WIDTH - 1, d), lambda o, i: (l, o, 0, 0))]
    in_specs = [rspec(d), pl.BlockSpec((nseg, CONV_WIDTH - 1, d), seg_map),
                pl.BlockSpec((tm, 3 * LANE), tab_map),
                *[wspec(w) for w in wts]]
    n_in = len(in_specs)
    in_specs += [pl.BlockSpec(memory_space=pl.ANY)] * len(prev_stacks)
    stack_out0 = 2
    return pl.pallas_call(
        kern,
        out_shape=out_shape,
        grid=(n_outer, n_inner),
        in_specs=in_specs,
        out_specs=out_specs,
        scratch_shapes=[pltpu.VMEM((nseg, CONV_WIDTH - 1, d), F32)],
        input_output_aliases={n_in + j: stack_out0 + j for j in range(len(prev_stacks))},
        compiler_params=pltpu.CompilerParams(
            dimension_semantics=("arbitrary", "arbitrary"),
            vmem_limit_bytes=VMEM_LIMIT_BYTES),
        name="proj",
    )(x2d, conv_init, tabs, *wts, *prev_stacks)


def _attn_prompt_kernel(q_ref, k_ref, v_ref, o_ref, *, tq, nq):
    i = pl.program_id(1)
    ncol = tq // LANE
    r_chunk = lax.broadcasted_iota(jnp.int32, (tq, LANE), 0) // CHUNK
    ones = jnp.ones((tq, LANE), BF16)

    def head_block(j, hh, masked, state):
        koff = j * tq
        q = q_ref[:, hh * HEAD_PAD:(hh + 1) * HEAD_PAD]
        k = k_ref[koff:koff + tq, hh * HEAD_PAD:(hh + 1) * HEAD_PAD]
        v = v_ref[koff:koff + tq, hh * LANE:(hh + 1) * LANE]
        s = _dot_nt(q, k)
        cols = []
        for c in range(ncol):
            sc = s[:, c * LANE:(c + 1) * LANE]
            if masked:
                c_chunk = (lax.broadcasted_iota(jnp.int32, (tq, LANE), 1) + c * LANE) // CHUNK
                sc = jnp.where(c_chunk <= r_chunk, sc, NEG_INF)
            cols.append(sc)
        m_cur = jnp.max(functools.reduce(jnp.maximum, cols), axis=1, keepdims=True)
        m_next = (jnp.broadcast_to(m_cur, (tq, LANE)) if state is None
                  else jnp.maximum(state[0], m_cur))
        p = jnp.concatenate([jnp.exp2(sc - m_next) for sc in cols], axis=1).astype(BF16)
        pv = _dot(p, jnp.concatenate([v, ones], axis=1))
        if state is None:
            return m_next, pv
        alpha = jnp.exp2(state[0] - m_next)
        return m_next, jnp.concatenate([alpha, alpha], axis=1) * state[1] + pv

    for qi in range(nq):
        @pl.when(i == qi)
        def _(qi=qi):
            state = [None] * N_HEADS
            for j in range(qi + 1):
                for hh in range(N_HEADS):
                    state[hh] = head_block(j, hh, j == qi, state[hh])
            for hh in range(N_HEADS):
                acc = state[hh][1]
                o_ref[:, hh * LANE:(hh + 1) * LANE] = (acc[:, :LANE] / acc[:, LANE:]).astype(BF16)


def _attn_prompt_call(qkv, *, b, s, tq):
    nq = s // tq
    wq = N_HEADS * HEAD_PAD
    wv = N_HEADS * LANE
    assert (2 * wq) % wv == 0
    return pl.pallas_call(
        functools.partial(_attn_prompt_kernel, tq=tq, nq=nq),
        out_shape=jax.ShapeDtypeStruct((b, s, N_HEADS * LANE), BF16),
        grid=(b, nq),
        in_specs=[pl.BlockSpec((None, tq, wq), lambda bb, i: (bb, i, 0)),
                  pl.BlockSpec((None, s, wq), lambda bb, i: (bb, 0, 1)),
                  pl.BlockSpec((None, s, wv), lambda bb, i: (bb, 0, 2 * wq // wv))],
        out_specs=pl.BlockSpec((None, tq, wv), lambda bb, i: (bb, i, 0)),
        compiler_params=pltpu.CompilerParams(
            dimension_semantics=("arbitrary", "arbitrary"),
            vmem_limit_bytes=VMEM_LIMIT_BYTES),
        name="attn_prompt",
    )(qkv, qkv, qkv)


def _attn_sample_kernel(q_ref, pckv_ref, pkpe_t_ref, nckv_ref, nkpe_ref, wuk_ref, wuv_ref,
                        o_ref, *, ls, rope, new_mask, nb):
    batches = range(nb)
    qas, qps = [], []
    for bi in batches:
        qa, qp = [], []
        for hh in range(N_HEADS):
            qn = q_ref[bi, :, hh * HEAD_PAD:hh * HEAD_PAD + LANE]
            qa.append(_dot_nt(qn, wuk_ref[:, hh * LANE:(hh + 1) * LANE]).astype(BF16))
            qp.append(q_ref[bi, :, hh * HEAD_PAD + LANE:hh * HEAD_PAD + LANE + rope])
        qas.append(jnp.concatenate(qa, axis=0))
        qps.append(jnp.concatenate(qp, axis=0))
    pckvs = [pckv_ref[bi].astype(BF16) for bi in batches]
    nckvs = [nckv_ref[bi].astype(BF16) for bi in batches]
    scores = []
    for bi in batches:
        s_past = _dot_nt(qas[bi], pckvs[bi]) + _dot(qps[bi], pkpe_t_ref[bi].astype(BF16))
        s_new = _dot_nt(qas[bi], nckvs[bi]) + _dot_nt(qps[bi], nkpe_ref[bi].astype(BF16))
        if new_mask is not None:
            r = lax.broadcasted_iota(jnp.int32, s_new.shape, 0) % ls
            c = lax.broadcasted_iota(jnp.int32, s_new.shape, 1)
            p0 = new_mask
            s_new = jnp.where((p0 + c) // CHUNK <= (p0 + r) // CHUNK, s_new, NEG_INF)
        scores.append((s_past, s_new))
    o_lats = []
    for bi, (s_past, s_new) in enumerate(scores):
        m = jnp.maximum(jnp.max(s_past, axis=1, keepdims=True),
                        jnp.max(s_new, axis=1, keepdims=True))
        p_past = jnp.exp2(s_past - m)
        p_new = jnp.exp2(s_new - m)
        denom = jnp.sum(p_past, axis=1, keepdims=True) + jnp.sum(p_new, axis=1, keepdims=True)
        o_lat = (_dot(p_past.astype(BF16), pckvs[bi]) + _dot(p_new.astype(BF16), nckvs[bi])) / denom
        o_lats.append(o_lat.astype(BF16))
    for bi, o_lat in enumerate(o_lats):
        for hh in range(N_HEADS):
            o_ref[bi, :, hh * LANE:(hh + 1) * LANE] = _dot(
                o_lat[hh * ls:(hh + 1) * ls], wuv_ref[:, hh * LANE:(hh + 1) * LANE]).astype(BF16)


def _attn_sample_call(q, cache_ckv, cache_kpe_t, nckv, nkpe, wuk, wuv, l, *, b, ls, new_mask):
    p, kv_lora = cache_ckv.shape[2], cache_ckv.shape[3]
    rope = cache_kpe_t.shape[2]
    nb = SAMPLE_BATCHES_PER_STEP if b % SAMPLE_BATCHES_PER_STEP == 0 else 1
    bmap = lambda bb: (bb, 0, 0)
    lmap = lambda bb: (l, bb, 0, 0)
    return pl.pallas_call(
        functools.partial(_attn_sample_kernel, ls=ls, rope=rope, new_mask=new_mask, nb=nb),
        out_shape=jax.ShapeDtypeStruct((b, ls, N_HEADS * LANE), BF16),
        grid=(b // nb,),
        in_specs=[pl.BlockSpec((nb, ls, N_HEADS * HEAD_PAD), bmap),
                  pl.BlockSpec((None, nb, p, kv_lora), lmap),
                  pl.BlockSpec((None, nb, rope, p), lmap),
                  pl.BlockSpec((None, nb, ls, kv_lora), lmap),
                  pl.BlockSpec((None, nb, ls, rope), lmap),
                  pl.BlockSpec((None,) + wuk.shape[1:], lambda bb: (l, 0, 0)),
                  pl.BlockSpec((None,) + wuv.shape[1:], lambda bb: (l, 0, 0))],
        out_specs=pl.BlockSpec((nb, ls, N_HEADS * LANE), bmap),
        compiler_params=pltpu.CompilerParams(
            dimension_semantics=("arbitrary",),
            vmem_limit_bytes=VMEM_LIMIT_BYTES),
        name="attn_sample",
    )(q, cache_ckv, cache_kpe_t, nckv, nkpe, wuk, wuv)


def _post_kernel(x_ref, yg_ref, yb_ref, wo_ref, w1_ref, w2_ref,
                 gpm_ref, gpf_ref, gqf_ref, o_ref, *, d_ff, ff_chunk, n_split):
    tm, d = x_ref.shape
    groups = [pl.ds(i * (tm // n_split), tm // n_split) for i in range(n_split)]
    n_chunks = d_ff // ff_chunk
    up = lambda h2, c: _dot(h2, w1_ref[:, c * ff_chunk:(c + 1) * ff_chunk])

    outs = []
    for r in groups:
        merged = yg_ref[r, :d].astype(F32) + yg_ref[r, d:].astype(F32) * yb_ref[r, :].astype(F32)
        outs.append(_dot(merged.astype(BF16), wo_ref[...]))
    x1s = [x_ref[r, :] + _rms(o, gpm_ref[...]) for r, o in zip(groups, outs)]
    h2s = [_rms(x1, gpf_ref[...]).astype(BF16) for x1 in x1s]
    a_next = [up(h2, 0) for h2 in h2s]
    accs = [jnp.zeros(x1.shape, F32) for x1 in x1s]
    for c in range(n_chunks):
        for i, h2 in enumerate(h2s):
            a = jnp.square(jnp.maximum(a_next[i], 0.0)).astype(BF16)
            if c + 1 < n_chunks:
                a_next[i] = up(h2, c + 1)
            accs[i] = accs[i] + _dot(a, w2_ref[c * ff_chunk:(c + 1) * ff_chunk, :])
    for r, x1, acc in zip(groups, x1s, accs):
        o_ref[r, :] = x1 + _rms(acc, gqf_ref[...])


def _post_call(x2d, yg, yb, wts, l, *, tm):
    rows, d = x2d.shape
    wo, w1, w2, gpm, gpf, gqf = wts
    d_ff = w1.shape[-1]
    row_map = lambda i: (i, 0)
    lay_map = lambda i: (l, 0, 0)

    def wspec(a):
        return pl.BlockSpec((None,) + a.shape[1:], lay_map, pipeline_mode=pl.Buffered(1))

    rspec = pl.BlockSpec((tm, d), row_map)
    return pl.pallas_call(
        functools.partial(_post_kernel, d_ff=d_ff, ff_chunk=min(d_ff, 1024),
                          n_split=POST_ROW_GROUPS if tm % (POST_ROW_GROUPS * 2 * SUBLANE) == 0 else 1),
        out_shape=jax.ShapeDtypeStruct((rows, d), F32),
        grid=(rows // tm,),
        in_specs=[rspec, pl.BlockSpec((tm, 2 * d), row_map), rspec, wspec(wo), wspec(w1), wspec(w2),
                  wspec(gpm), wspec(gpf), wspec(gqf)],
        out_specs=rspec,
        compiler_params=pltpu.CompilerParams(
            dimension_semantics=("arbitrary",),
            vmem_limit_bytes=VMEM_LIMIT_BYTES),
        name="post",
    )(x2d, yg, yb, wo, w1, w2, gpm, gpf, gqf)


def _rope_tables(pos, rope):
    inv = ROPE_THETA ** (-jnp.arange(0, rope, 2, dtype=F32) / rope)
    ang = pos.astype(F32)[:, None] * inv[None, :]
    cos, sin = jnp.cos(ang), jnp.sin(ang)
    z = jnp.zeros_like(cos)
    pad = jnp.zeros((pos.shape[0], LANE - rope), F32)
    cos_t = jnp.concatenate([cos, cos, pad], axis=1)
    sin_a = jnp.concatenate([z, sin, pad], axis=1)
    sin_b = jnp.concatenate([-sin, z, pad], axis=1)
    return jnp.concatenate([cos_t, sin_a, sin_b], axis=1)


def _prep_weights(w_in, conv_w, w_uq, w_uk, w_uv, w_o, w1, w2, g_pre_mix, g_post_mix, g_pre_ffn,
                  g_post_ffn, g_q, g_kv, *, nope, rope):
    depth, d, _ = w_in.shape
    q_lora = w_uq.shape[1]
    off_ga = w_in.shape[-1] - 2 * d
    w_a = jnp.pad(w_in[:, :, d:off_ga],
                  ((0, 0), (0, 0), (0, LANE - rope))).astype(BF16)
    w_b = w_in[:, :, :d].astype(BF16)
    w_g = w_in[:, :, off_ga:].astype(BF16)
    uq = w_uq.reshape(depth, q_lora, N_HEADS, nope + rope)
    wuq = jnp.concatenate(
        [uq[..., :nope].reshape(depth, q_lora, N_HEADS * nope),
         uq[..., nope:].reshape(depth, q_lora, N_HEADS * rope)], axis=2).astype(BF16)
    wuk, wuv = w_uk.astype(BF16), w_uv.astype(BF16)
    g3 = lambda g: g[:, None, :]
    proj_w = (w_a, w_b, w_g, wuq, wuk, wuv, conv_w, g3(g_pre_mix), g3(g_q), g3(g_kv))
    post_w = (w_o.astype(BF16), w1.astype(BF16), w2.astype(BF16),
              g3(g_post_mix), g3(g_pre_ffn), g3(g_post_ffn))
    return proj_w, post_w, wuk, wuv


def _tile(n, pref):
    t = min(n, pref)
    assert n % t == 0, (n, t)
    return t


def kernel(x_prompt, x_sample, cache_ckv, cache_kpe, state_conv, w_in, conv_w, w_uq, w_uk, w_uv,
           w_o, w1, w2, g_pre_mix, g_post_mix, g_pre_ffn, g_post_ffn, g_q, g_kv):
    b, s, d = x_prompt.shape
    bs, ls, _ = x_sample.shape
    depth, _, p, kv_lora = cache_ckv.shape
    rope = cache_kpe.shape[-1]
    q_lora = g_q.shape[-1]
    nope = w_uk.shape[-1] // N_HEADS
    v_dim = w_uv.shape[-1] // N_HEADS
    assert nope == LANE and v_dim == LANE and rope * 2 == LANE and d == N_HEADS * v_dim
    assert conv_w.shape[1] == CONV_WIDTH and s % CHUNK == 0 and ls >= CONV_WIDTH - 1
    q_scale = float(nope + rope) ** -0.5 * float(np.log2(np.e))
    dims = (d, q_lora, kv_lora, rope, q_scale)

    assert q_lora % LANE == 0 and kv_lora % LANE == 0
    proj_w, post_w, wuk, wuv = _prep_weights(
        w_in, conv_w, w_uq, w_uk, w_uv, w_o, w1, w2, g_pre_mix, g_post_mix, g_pre_ffn,
        g_post_ffn, g_q, g_kv, nope=nope, rope=rope)
    cache_kpe_t = jnp.swapaxes(cache_kpe, 2, 3)

    tm_p = _tile(s, ROW_TILE)
    tq = _tile(s, ROW_TILE)
    segs_per_tile = max(1, min(bs, ROW_TILE // ls))
    assert bs % segs_per_tile == 0
    tm_s = segs_per_tile * ls
    tabs_p = _rope_tables(jnp.arange(s, dtype=jnp.int32), rope)
    tabs_s = jnp.tile(_rope_tables(p + jnp.arange(ls, dtype=jnp.int32), rope), (segs_per_tile, 1))
    pos_new = p + np.arange(ls)
    new_all_visible = bool(np.all((pos_new[None, :] // CHUNK) <= (pos_new[:, None] // CHUNK)))
    new_mask = None if new_all_visible else p

    xp = x_prompt.reshape(b * s, d)
    xs = x_sample.reshape(bs * ls, d)
    zero_buf = jnp.zeros((b, CONV_WIDTH - 1, d), F32)
    stk_p, stk_s = (), ()
    for l in range(depth):
        yg, qkv, *stk_p = _proj_call(
            xp, zero_buf, tabs_p, proj_w, l, stk_p, n_outer=b, n_inner=s // tm_p, tm=tm_p,
            seg=tm_p, dims=dims, expand_kv=True)
        yb = _attn_prompt_call(qkv.reshape(b, s, -1), b=b, s=s, tq=tq)
        xp = _post_call(xp, yg, yb.reshape(b * s, d), post_w, l, tm=tm_p)

        yg, q, *stk_s = _proj_call(
            xs, state_conv[l], tabs_s, proj_w, l, stk_s, n_outer=bs // segs_per_tile, n_inner=1,
            tm=tm_s, seg=ls, dims=dims, expand_kv=False)
        yb = _attn_sample_call(q.reshape(bs, ls, -1), cache_ckv, cache_kpe_t,
                               stk_s[0].reshape(depth, bs, ls, kv_lora),
                               stk_s[1].reshape(depth, bs, ls, rope),
                               wuk, wuv, l, b=bs, ls=ls, new_mask=new_mask)
        xs = _post_call(xs, yg, yb.reshape(bs * ls, d), post_w, l, tm=tm_s)

    return (xp.reshape(b, s, d), xs.reshape(bs, ls, d),
            stk_p[0].reshape(depth, b, s, kv_lora), stk_p[1].reshape(depth, b, s, rope), stk_p[2],
            stk_s[0].reshape(depth, bs, ls, kv_lora), stk_s[1].reshape(depth, bs, ls, rope),
            stk_s[2])
```

```python
import functools

import numpy as np
import jax
import jax.numpy as jnp
from jax import lax
from jax.experimental import pallas as pl
from jax.experimental.pallas import tpu as pltpu

CHUNK = 64
CONV_WIDTH = 3
N_HEADS = 8
ROPE_THETA = 10000.0
EPS = 1e-6
NEG_INF = -1e30

LANE = 128
SUBLANE = 8
HEAD_PAD = 2 * LANE
VMEM_LIMIT_BYTES = 56 * 1024 * 1024
ROW_TILE = 512
POST_ROW_GROUPS = 2
PROJ_GROUP_EIGHTHS = (3, 5)
SAMPLE_BATCHES_PER_STEP = 2

F32 = jnp.float32
BF16 = jnp.bfloat16


def _rms(x, g):
    ms = jnp.mean(x * x, axis=-1, keepdims=True)
    return x * lax.rsqrt(ms + EPS) * g


def _rope(pe, cos_t, sin_a, sin_b):
    return (pe * cos_t + pltpu.roll(pe, LANE // 4, 1) * sin_a
            + pltpu.roll(pe, LANE - LANE // 4, 1) * sin_b)


def _dot(a, b):
    return jnp.dot(a, b, preferred_element_type=F32)


def _dot_nt(a, b):
    return lax.dot_general(a, b, (((1,), (1,)), ((), ())), preferred_element_type=F32)


def _proj_kernel(x_ref, init_ref, tab_ref,
                 wa_ref, wb_ref, wg_ref, wuq_ref, wuk_ref, wuv_ref, convw_ref,
                 gpre_ref, gq_ref, gkv_ref, *rest,
                 d, seg, group_rows, q_lora, kv_lora, rope, sm_scale, expand_kv, n_prev):
    yg_ref, qkv_ref, ckv_ref, kpe_ref, cbuf_ref, carry_ref = rest[n_prev:]
    off_k = N_HEADS * HEAD_PAD
    off_v = 2 * off_k
    inner = pl.program_id(1)
    group_off = [sum(group_rows[:g]) for g in range(len(group_rows))]
    groups = [pl.ds(o, z) for o, z in zip(group_off, group_rows)]
    off_q = 2 * d
    off_kv = q_lora
    off_pe = off_kv + kv_lora

    @pl.when(inner == 0)
    def _():
        carry_ref[...] = init_ref[...]

    hs = [_rms(x_ref[r, :], gpre_ref[...]).astype(BF16) for r in groups]
    tabs = [tuple(tab_ref[r, t * LANE:(t + 1) * LANE] for t in range(3)) for r in groups]
    lats = [_dot(h, wa_ref[:, off_q:]) for h in hs]
    zcs = [_dot(h, wa_ref[:, :d]) for h in hs]

    for r, lat, tab in zip(groups, lats, tabs):
        cqn = _rms(lat[:, :off_kv], gq_ref[...]).astype(BF16)
        qn = _dot(cqn, wuq_ref[:, :N_HEADS * LANE]) * sm_scale
        qp = _dot(cqn, wuq_ref[:, N_HEADS * LANE:]) * sm_scale
        for hh in range(N_HEADS):
            pe = qp[:, (hh // 2) * LANE:(hh // 2 + 1) * LANE]
            if hh % 2:
                pe = pltpu.roll(pe, LANE // 2, 1)
            qkv_ref[r, hh * HEAD_PAD:hh * HEAD_PAD + LANE] = qn[:, hh * LANE:(hh + 1) * LANE].astype(BF16)
            qkv_ref[r, hh * HEAD_PAD + LANE:(hh + 1) * HEAD_PAD] = _rope(pe, *tab).astype(BF16)

    for r, lat, tab in zip(groups, lats, tabs):
        ckv = _rms(lat[:, off_kv:off_pe], gkv_ref[...])
        ckv_ref[r, :] = ckv
        kpe = _rope(lat[:, off_pe:], *tab)
        kpe_ref[r, :] = kpe[:, :rope]
        if expand_kv:
            ckvn = ckv.astype(BF16)
            kpe_b = kpe.astype(BF16)
            for hp in range(N_HEADS // 2):
                kn = _dot(ckvn, wuk_ref[:, hp * HEAD_PAD:(hp + 1) * HEAD_PAD]).astype(BF16)
                base = off_k + 2 * hp * HEAD_PAD
                qkv_ref[r, base:base + LANE] = kn[:, :LANE]
                qkv_ref[r, base + LANE:base + HEAD_PAD] = kpe_b
                qkv_ref[r, base + HEAD_PAD:base + HEAD_PAD + LANE] = kn[:, LANE:]
                qkv_ref[r, base + HEAD_PAD + LANE:base + 2 * HEAD_PAD] = kpe_b
            qkv_ref[r, off_v:] = _dot(ckvn, wuv_ref[...]).astype(BF16)

    for r, h in zip(groups, hs):
        yg_ref[r, d:] = jax.nn.sigmoid(_dot(h, wg_ref[:, d:])).astype(BF16)

    us = [zc * _dot(h, wa_ref[:, d:off_q]) for zc, h in zip(zcs, hs)]
    w0 = convw_ref[0:1, :]
    w1 = convw_ref[1:2, :]
    w2 = convw_ref[2:3, :]
    rows = lax.broadcasted_iota(jnp.int32, (SUBLANE, 1), 0)
    ycs, tail = [], None
    for g, u in enumerate(us):
        parts = []
        pm = min(seg, group_rows[g])
        for p in range(group_rows[g] // pm):
            row0 = group_off[g] + p * pm
            sidx = row0 // seg
            up = u[p * pm:(p + 1) * pm]
            init = carry_ref[sidx] if row0 % seg == 0 else tail
            i0, i1 = init[0:1], init[1:2]
            r1, r2 = pltpu.roll(up, 1, 0), pltpu.roll(up, 2, 0)
            h1 = jnp.where(rows == 0, i1, r1[:SUBLANE])
            h2 = jnp.where(rows == 0, i0, jnp.where(rows == 1, i1, r2[:SUBLANE]))
            u1 = jnp.concatenate([h1, r1[SUBLANE:]], axis=0)
            u2 = jnp.concatenate([h2, r2[SUBLANE:]], axis=0)
            parts.append(w0 * u2 + w1 * u1 + w2 * up)
            tail = up[pm - (CONV_WIDTH - 1):pm]
            if (row0 + pm) % seg == 0:
                carry_ref[sidx] = tail
                cbuf_ref[sidx] = tail
        ycs.append(parts[0] if len(parts) == 1 else jnp.concatenate(parts, axis=0))

    gas = [_dot(h, wg_ref[:, :d]) for h in hs]
    zbs = [_dot(h, wb_ref[...]) for h in hs]
    for r, ga, zb, yc in zip(groups, gas, zbs, ycs):
        yg_ref[r, :d] = (jax.nn.sigmoid(ga) * zb * yc).astype(BF16)


def _proj_call(x2d, conv_init, tabs, wts, l, prev_stacks, *, n_outer, n_inner, tm, seg, dims,
               expand_kv):
    d, q_lora, kv_lora, rope, sm_scale = dims
    depth = wts[0].shape[0]
    nseg = tm // seg
    rows = x2d.shape[0]
    row_map = lambda o, i: (o * n_inner + i, 0)
    tab_map = lambda o, i: (i, 0)
    seg_map = lambda o, i: (o, 0, 0)
    lay_map = lambda o, i: (l, 0, 0)
    lrow_map = lambda o, i: (l, o * n_inner + i, 0)

    def wspec(a):
        return pl.BlockSpec((None,) + a.shape[1:], lay_map, pipeline_mode=pl.Buffered(1))

    def rspec(n):
        return pl.BlockSpec((tm, n), row_map)

    group_rows = tuple(tm * f // 8 for f in PROJ_GROUP_EIGHTHS)
    if sum(group_rows) != tm or any(z % (2 * SUBLANE) or (z % seg and seg % tm) for z in group_rows):
        group_rows = (tm,)
    kern = functools.partial(_proj_kernel, d=d, seg=seg, group_rows=group_rows, q_lora=q_lora,
                             kv_lora=kv_lora, rope=rope, sm_scale=sm_scale,
                             expand_kv=expand_kv, n_prev=len(prev_stacks))
    qkv_w = N_HEADS * (2 * HEAD_PAD + LANE) if expand_kv else N_HEADS * HEAD_PAD
    out_shape = [
        jax.ShapeDtypeStruct((rows, 2 * d), BF16),
        jax.ShapeDtypeStruct((rows, qkv_w), BF16),
        jax.ShapeDtypeStruct((depth, rows, kv_lora), F32),
        jax.ShapeDtypeStruct((depth, rows, rope), F32),
        jax.ShapeDtypeStruct((depth,) + conv_init.shape, F32),
    ]
    out_specs = [rspec(2 * d), rspec(qkv_w),
                 pl.BlockSpec((None, tm, kv_lora), lrow_map),
                 pl.BlockSpec((None, tm, rope), lrow_map),
                 pl.BlockSpec((None, nseg, CONV_WIDTH - 1, d), lambda o, i: (l, o, 0, 0))]
    in_specs = [rspec(d), pl.BlockSpec((nseg, CONV_WIDTH - 1, d), seg_map),
                pl.BlockSpec((tm, 3 * LANE), tab_map),
                *[wspec(w) for w in wts]]
    n_in = len(in_specs)
    in_specs += [pl.BlockSpec(memory_space=pl.ANY)] * len(prev_stacks)
    stack_out0 = 2
    return pl.pallas_call(
        kern,
        out_shape=out_shape,
        grid=(n_outer, n_inner),
        in_specs=in_specs,
        out_specs=out_specs,
        scratch_shapes=[pltpu.VMEM((nseg, CONV_WIDTH - 1, d), F32)],
        input_output_aliases={n_in + j: stack_out0 + j for j in range(len(prev_stacks))},
        compiler_params=pltpu.CompilerParams(
            dimension_semantics=("arbitrary", "arbitrary"),
            vmem_limit_bytes=VMEM_LIMIT_BYTES),
        name="proj",
    )(x2d, conv_init, tabs, *wts, *prev_stacks)


def _attn_prompt_kernel(q_ref, k_ref, v_ref, o_ref, *, tq, nq):
    i = pl.program_id(1)
    ncol = tq // LANE
    r_chunk = lax.broadcasted_iota(jnp.int32, (tq, LANE), 0) // CHUNK
    ones = jnp.ones((tq, LANE), BF16)

    def head_block(j, hh, masked, state):
        koff = j * tq
        q = q_ref[:, hh * HEAD_PAD:(hh + 1) * HEAD_PAD]
        k = k_ref[koff:koff + tq, hh * HEAD_PAD:(hh + 1) * HEAD_PAD]
        v = v_ref[koff:koff + tq, hh * LANE:(hh + 1) * LANE]
        s = _dot_nt(q, k)
        cols = []
        for c in range(ncol):
            sc = s[:, c * LANE:(c + 1) * LANE]
            if masked:
                c_chunk = (lax.broadcasted_iota(jnp.int32, (tq, LANE), 1) + c * LANE) // CHUNK
                sc = jnp.where(c_chunk <= r_chunk, sc, NEG_INF)
            cols.append(sc)
        m_cur = jnp.max(functools.reduce(jnp.maximum, cols), axis=1, keepdims=True)
        m_next = (jnp.broadcast_to(m_cur, (tq, LANE)) if state is None
                  else jnp.maximum(state[0], m_cur))
        p = jnp.concatenate([jnp.exp2(sc - m_next) for sc in cols], axis=1).astype(BF16)
        pv = _dot(p, jnp.concatenate([v, ones], axis=1))
        if state is None:
            return m_next, pv
        alpha = jnp.exp2(state[0] - m_next)
        return m_next, jnp.concatenate([alpha, alpha], axis=1) * state[1] + pv

    for qi in range(nq):
        @pl.when(i == qi)
        def _(qi=qi):
            state = [None] * N_HEADS
            for j in range(qi + 1):
                for hh in range(N_HEADS):
                    state[hh] = head_block(j, hh, j == qi, state[hh])
            for hh in range(N_HEADS):
                acc = state[hh][1]
                o_ref[:, hh * LANE:(hh + 1) * LANE] = (acc[:, :LANE] / acc[:, LANE:]).astype(BF16)


def _attn_prompt_call(qkv, *, b, s, tq):
    nq = s // tq
    wq = N_HEADS * HEAD_PAD
    wv = N_HEADS * LANE
    assert (2 * wq) % wv == 0
    return pl.pallas_call(
        functools.partial(_attn_prompt_kernel, tq=tq, nq=nq),
        out_shape=jax.ShapeDtypeStruct((b, s, N_HEADS * LANE), BF16),
        grid=(b, nq),
        in_specs=[pl.BlockSpec((None, tq, wq), lambda bb, i: (bb, i, 0)),
                  pl.BlockSpec((None, s, wq), lambda bb, i: (bb, 0, 1)),
                  pl.BlockSpec((None, s, wv), lambda bb, i: (bb, 0, 2 * wq // wv))],
        out_specs=pl.BlockSpec((None, tq, wv), lambda bb, i: (bb, i, 0)),
        compiler_params=pltpu.CompilerParams(
            dimension_semantics=("arbitrary", "arbitrary"),
            vmem_limit_bytes=VMEM_LIMIT_BYTES),
        name="attn_prompt",
    )(qkv, qkv, qkv)


def _attn_sample_kernel(q_ref, pckv_ref, pkpe_t_ref, nckv_ref, nkpe_ref, wuk_ref, wuv_ref,
                        o_ref, *, ls, rope, new_mask, nb):
    batches = range(nb)
    qas, qps = [], []
    for bi in batches:
        qa, qp = [], []
        for hh in range(N_HEADS):
            qn = q_ref[bi, :, hh * HEAD_PAD:hh * HEAD_PAD + LANE]
            qa.append(_dot_nt(qn, wuk_ref[:, hh * LANE:(hh + 1) * LANE]).astype(BF16))
            qp.append(q_ref[bi, :, hh * HEAD_PAD + LANE:hh * HEAD_PAD + LANE + rope])
        qas.append(jnp.concatenate(qa, axis=0))
        qps.append(jnp.concatenate(qp, axis=0))
    pckvs = [pckv_ref[bi].astype(BF16) for bi in batches]
    nckvs = [nckv_ref[bi].astype(BF16) for bi in batches]
    scores = []
    for bi in batches:
        s_past = _dot_nt(qas[bi], pckvs[bi]) + _dot(qps[bi], pkpe_t_ref[bi].astype(BF16))
        s_new = _dot_nt(qas[bi], nckvs[bi]) + _dot_nt(qps[bi], nkpe_ref[bi].astype(BF16))
        if new_mask is not None:
            r = lax.broadcasted_iota(jnp.int32, s_new.shape, 0) % ls
            c = lax.broadcasted_iota(jnp.int32, s_new.shape, 1)
            p0 = new_mask
            s_new = jnp.where((p0 + c) // CHUNK <= (p0 + r) // CHUNK, s_new, NEG_INF)
        scores.append((s_past, s_new))
    o_lats = []
    for bi, (s_past, s_new) in enumerate(scores):
        m = jnp.maximum(jnp.max(s_past, axis=1, keepdims=True),
                        jnp.max(s_new, axis=1, keepdims=True))
        p_past = jnp.exp2(s_past - m)
        p_new = jnp.exp2(s_new - m)
        denom = jnp.sum(p_past, axis=1, keepdims=True) + jnp.sum(p_new, axis=1, keepdims=True)
        o_lat = (_dot(p_past.astype(BF16), pckvs[bi]) + _dot(p_new.astype(BF16), nckvs[bi])) / denom
        o_lats.append(o_lat.astype(BF16))
    for bi, o_lat in enumerate(o_lats):
        for hh in range(N_HEADS):
            o_ref[bi, :, hh * LANE:(hh + 1) * LANE] = _dot(
                o_lat[hh * ls:(hh + 1) * ls], wuv_ref[:, hh * LANE:(hh + 1) * LANE]).astype(BF16)


def _attn_sample_call(q, cache_ckv, cache_kpe_t, nckv, nkpe, wuk, wuv, l, *, b, ls, new_mask):
    p, kv_lora = cache_ckv.shape[2], cache_ckv.shape[3]
    rope = cache_kpe_t.shape[2]
    nb = SAMPLE_BATCHES_PER_STEP if b % SAMPLE_BATCHES_PER_STEP == 0 else 1
    bmap = lambda bb: (bb, 0, 0)
    lmap = lambda bb: (l, bb, 0, 0)
    return pl.pallas_call(
        functools.partial(_attn_sample_kernel, ls=ls, rope=rope, new_mask=new_mask, nb=nb),
        out_shape=jax.ShapeDtypeStruct((b, ls, N_HEADS * LANE), BF16),
        grid=(b // nb,),
        in_specs=[pl.BlockSpec((nb, ls, N_HEADS * HEAD_PAD), bmap),
                  pl.BlockSpec((None, nb, p, kv_lora), lmap),
                  pl.BlockSpec((None, nb, rope, p), lmap),
                  pl.BlockSpec((None, nb, ls, kv_lora), lmap),
                  pl.BlockSpec((None, nb, ls, rope), lmap),
                  pl.BlockSpec((None,) + wuk.shape[1:], lambda bb: (l, 0, 0)),
                  pl.BlockSpec((None,) + wuv.shape[1:], lambda bb: (l, 0, 0))],
        out_specs=pl.BlockSpec((nb, ls, N_HEADS * LANE), bmap),
        compiler_params=pltpu.CompilerParams(
            dimension_semantics=("arbitrary",),
            vmem_limit_bytes=VMEM_LIMIT_BYTES),
        name="attn_sample",
    )(q, cache_ckv, cache_kpe_t, nckv, nkpe, wuk, wuv)


def _post_kernel(x_ref, yg_ref, yb_ref, wo_ref, w1_ref, w2_ref,
                 gpm_ref, gpf_ref, gqf_ref, o_ref, *, d_ff, ff_chunk, n_split):
    tm, d = x_ref.shape
    groups = [pl.ds(i * (tm // n_split), tm // n_split) for i in range(n_split)]
    n_chunks = d_ff // ff_chunk
    up = lambda h2, c: _dot(h2, w1_ref[:, c * ff_chunk:(c + 1) * ff_chunk])

    outs = []
    for r in groups:
        merged = yg_ref[r, :d].astype(F32) + yg_ref[r, d:].astype(F32) * yb_ref[r, :].astype(F32)
        outs.append(_dot(merged.astype(BF16), wo_ref[...]))
    x1s = [x_ref[r, :] + _rms(o, gpm_ref[...]) for r, o in zip(groups, outs)]
    h2s = [_rms(x1, gpf_ref[...]).astype(BF16) for x1 in x1s]
    a_next = [up(h2, 0) for h2 in h2s]
    accs = [jnp.zeros(x1.shape, F32) for x1 in x1s]
    for c in range(n_chunks):
        for i, h2 in enumerate(h2s):
            a = jnp.square(jnp.maximum(a_next[i], 0.0)).astype(BF16)
            if c + 1 < n_chunks:
                a_next[i] = up(h2, c + 1)
            accs[i] = accs[i] + _dot(a, w2_ref[c * ff_chunk:(c + 1) * ff_chunk, :])
    for r, x1, acc in zip(groups, x1s, accs):
        o_ref[r, :] = x1 + _rms(acc, gqf_ref[...])


def _post_call(x2d, yg, yb, wts, l, *, tm):
    rows, d = x2d.shape
    wo, w1, w2, gpm, gpf, gqf = wts
    d_ff = w1.shape[-1]
    row_map = lambda i: (i, 0)
    lay_map = lambda i: (l, 0, 0)

    def wspec(a):
        return pl.BlockSpec((None,) + a.shape[1:], lay_map, pipeline_mode=pl.Buffered(1))

    rspec = pl.BlockSpec((tm, d), row_map)
    return pl.pallas_call(
        functools.partial(_post_kernel, d_ff=d_ff, ff_chunk=min(d_ff, 1024),
                          n_split=POST_ROW_GROUPS if tm % (POST_ROW_GROUPS * 2 * SUBLANE) == 0 else 1),
        out_shape=jax.ShapeDtypeStruct((rows, d), F32),
        grid=(rows // tm,),
        in_specs=[rspec, pl.BlockSpec((tm, 2 * d), row_map), rspec, wspec(wo), wspec(w1), wspec(w2),
                  wspec(gpm), wspec(gpf), wspec(gqf)],
        out_specs=rspec,
        compiler_params=pltpu.CompilerParams(
            dimension_semantics=("arbitrary",),
            vmem_limit_bytes=VMEM_LIMIT_BYTES),
        name="post",
    )(x2d, yg, yb, wo, w1, w2, gpm, gpf, gqf)


def _rope_tables(pos, rope):
    inv = ROPE_THETA ** (-jnp.arange(0, rope, 2, dtype=F32) / rope)
    ang = pos.astype(F32)[:, None] * inv[None, :]
    cos, sin = jnp.cos(ang), jnp.sin(ang)
    z = jnp.zeros_like(cos)
    pad = jnp.zeros((pos.shape[0], LANE - rope), F32)
    cos_t = jnp.concatenate([cos, cos, pad], axis=1)
    sin_a = jnp.concatenate([z, sin, pad], axis=1)
    sin_b = jnp.concatenate([-sin, z, pad], axis=1)
    return jnp.concatenate([cos_t, sin_a, sin_b], axis=1)


def _prep_weights(w_in, conv_w, w_uq, w_uk, w_uv, w_o, w1, w2, g_pre_mix, g_post_mix, g_pre_ffn,
                  g_post_ffn, g_q, g_kv, *, nope, rope):
    depth, d, _ = w_in.shape
    q_lora = w_uq.shape[1]
    off_ga = w_in.shape[-1] - 2 * d
    w_a = jnp.pad(w_in[:, :, d:off_ga],
                  ((0, 0), (0, 0), (0, LANE - rope))).astype(BF16)
    w_b = w_in[:, :, :d].astype(BF16)
    w_g = w_in[:, :, off_ga:].astype(BF16)
    uq = w_uq.reshape(depth, q_lora, N_HEADS, nope + rope)
    wuq = jnp.concatenate(
        [uq[..., :nope].reshape(depth, q_lora, N_HEADS * nope),
         uq[..., nope:].reshape(depth, q_lora, N_HEADS * rope)], axis=2).astype(BF16)
    wuk, wuv = w_uk.astype(BF16), w_uv.astype(BF16)
    g3 = lambda g: g[:, None, :]
    proj_w = (w_a, w_b, w_g, wuq, wuk, wuv, conv_w, g3(g_pre_mix), g3(g_q), g3(g_kv))
    post_w = (w_o.astype(BF16), w1.astype(BF16), w2.astype(BF16),
              g3(g_post_mix), g3(g_pre_ffn), g3(g_post_ffn))
    return proj_w, post_w, wuk, wuv


def _tile(n, pref):
    t = min(n, pref)
    assert n % t == 0, (n, t)
    return t


def kernel(x_prompt, x_sample, cache_ckv, cache_kpe, state_conv, w_in, conv_w, w_uq, w_uk, w_uv,
           w_o, w1, w2, g_pre_mix, g_post_mix, g_pre_ffn, g_post_ffn, g_q, g_kv):
    b, s, d = x_prompt.shape
    bs, ls, _ = x_sample.shape
    depth, _, p, kv_lora = cache_ckv.shape
    rope = cache_kpe.shape[-1]
    q_lora = g_q.shape[-1]
    nope = w_uk.shape[-1] // N_HEADS
    v_dim = w_uv.shape[-1] // N_HEADS
    assert nope == LANE and v_dim == LANE and rope * 2 == LANE and d == N_HEADS * v_dim
    assert conv_w.shape[1] == CONV_WIDTH and s % CHUNK == 0 and ls >= CONV_WIDTH - 1
    q_scale = float(nope + rope) ** -0.5 * float(np.log2(np.e))
    dims = (d, q_lora, kv_lora, rope, q_scale)

    assert q_lora % LANE == 0 and kv_lora % LANE == 0
    proj_w, post_w, wuk, wuv = _prep_weights(
        w_in, conv_w, w_uq, w_uk, w_uv, w_o, w1, w2, g_pre_mix, g_post_mix, g_pre_ffn,
        g_post_ffn, g_q, g_kv, nope=nope, rope=rope)
    cache_kpe_t = jnp.swapaxes(cache_kpe, 2, 3)

    tm_p = _tile(s, ROW_TILE)
    tq = _tile(s, ROW_TILE)
    segs_per_tile = max(1, min(bs, ROW_TILE // ls))
    assert bs % segs_per_tile == 0
    tm_s = segs_per_tile * ls
    tabs_p = _rope_tables(jnp.arange(s, dtype=jnp.int32), rope)
    tabs_s = jnp.tile(_rope_tables(p + jnp.arange(ls, dtype=jnp.int32), rope), (segs_per_tile, 1))
    pos_new = p + np.arange(ls)
    new_all_visible = bool(np.all((pos_new[None, :] // CHUNK) <= (pos_new[:, None] // CHUNK)))
    new_mask = None if new_all_visible else p

    xp = x_prompt.reshape(b * s, d)
    xs = x_sample.reshape(bs * ls, d)
    zero_buf = jnp.zeros((b, CONV_WIDTH - 1, d), F32)
    stk_p, stk_s = (), ()
    for l in range(depth):
        yg, qkv, *stk_p = _proj_call(
            xp, zero_buf, tabs_p, proj_w, l, stk_p, n_outer=b, n_inner=s // tm_p, tm=tm_p,
            seg=tm_p, dims=dims, expand_kv=True)
        yb = _attn_prompt_call(qkv.reshape(b, s, -1), b=b, s=s, tq=tq)
        xp = _post_call(xp, yg, yb.reshape(b * s, d), post_w, l, tm=tm_p)

        yg, q, *stk_s = _proj_call(
            xs, state_conv[l], tabs_s, proj_w, l, stk_s, n_outer=bs // segs_per_tile, n_inner=1,
            tm=tm_s, seg=ls, dims=dims, expand_kv=False)
        yb = _attn_sample_call(q.reshape(bs, ls, -1), cache_ckv, cache_kpe_t,
                               stk_s[0].reshape(depth, bs, ls, kv_lora),
                               stk_s[1].reshape(depth, bs, ls, rope),
                               wuk, wuv, l, b=bs, ls=ls, new_mask=new_mask)
        xs = _post_call(xs, yg, yb.reshape(bs * ls, d), post_w, l, tm=tm_s)

    return (xp.reshape(b, s, d), xs.reshape(bs, ls, d),
            stk_p[0].reshape(depth, b, s, kv_lora), stk_p[1].reshape(depth, b, s, rope), stk_p[2],
            stk_s[0].reshape(depth, bs, ls, kv_lora), stk_s[1].reshape(depth, bs, ls, rope),
            stk_s[2])
```

```python
import functools

import numpy as np
import jax
import jax.numpy as jnp
from jax import lax
from jax.experimental import pallas as pl
from jax.experimental.pallas import tpu as pltpu

CHUNK = 64
CONV_WIDTH = 3
N_HEADS = 8
ROPE_THETA = 10000.0
EPS = 1e-6
NEG_INF = -1e30

LANE = 128
SUBLANE = 8
HEAD_PAD = 2 * LANE
VMEM_LIMIT_BYTES = 56 * 1024 * 1024
ROW_TILE = 512
POST_GROUP_EIGHTHS = (5, 3)
PROJ_GROUP_EIGHTHS = (3, 5)
SAMPLE_BATCHES_PER_STEP = 4

F32 = jnp.float32
BF16 = jnp.bfloat16


def _rms(x, g):
    ms = jnp.mean(x * x, axis=-1, keepdims=True)
    return x * lax.rsqrt(ms + EPS) * g


def _rope(pe, cos_t, sin_a, sin_b):
    return (pe * cos_t + pltpu.roll(pe, LANE // 4, 1) * sin_a
            + pltpu.roll(pe, LANE - LANE // 4, 1) * sin_b)


def _dot(a, b):
    return jnp.dot(a, b, preferred_element_type=F32)


def _dot_nt(a, b):
    return lax.dot_general(a, b, (((1,), (1,)), ((), ())), preferred_element_type=F32)


def _proj_kernel(x_ref, init_ref, tab_ref,
                 wa_ref, wb_ref, wg_ref, wuq_ref, wuk_ref, wuv_ref, convw_ref,
                 gpre_ref, gq_ref, gkv_ref, *rest,
                 d, seg, group_rows, q_lora, kv_lora, rope, sm_scale, expand_kv, n_prev):
    yg_ref, qkv_ref, ckv_ref, kpe_ref, cbuf_ref, carry_ref = rest[n_prev:]
    off_k = N_HEADS * HEAD_PAD
    off_v = 2 * off_k
    inner = pl.program_id(1)
    group_off = [sum(group_rows[:g]) for g in range(len(group_rows))]
    groups = [pl.ds(o, z) for o, z in zip(group_off, group_rows)]
    off_q = 2 * d
    off_kv = q_lora
    off_pe = off_kv + kv_lora

    @pl.when(inner == 0)
    def _():
        carry_ref[...] = init_ref[...]

    hs = [_rms(x_ref[r, :], gpre_ref[...]).astype(BF16) for r in groups]
    tabs = [tuple(tab_ref[r, t * LANE:(t + 1) * LANE] for t in range(3)) for r in groups]
    lats = [_dot(h, wa_ref[:, off_q:]) for h in hs]
    zcs = [_dot(h, wa_ref[:, :d]) for h in hs]

    for r, lat, tab in zip(groups, lats, tabs):
        cqn = _rms(lat[:, :off_kv], gq_ref[...]).astype(BF16)
        qn = _dot(cqn, wuq_ref[:, :N_HEADS * LANE]) * sm_scale
        qp = _dot(cqn, wuq_ref[:, N_HEADS * LANE:]) * sm_scale
        for hh in range(N_HEADS):
            pe = qp[:, (hh // 2) * LANE:(hh // 2 + 1) * LANE]
            if hh % 2:
                pe = pltpu.roll(pe, LANE // 2, 1)
            qkv_ref[r, hh * HEAD_PAD:hh * HEAD_PAD + LANE] = qn[:, hh * LANE:(hh + 1) * LANE].astype(BF16)
            qkv_ref[r, hh * HEAD_PAD + LANE:(hh + 1) * HEAD_PAD] = _rope(pe, *tab).astype(BF16)

    for r, lat, tab in zip(groups, lats, tabs):
        ckv = _rms(lat[:, off_kv:off_pe], gkv_ref[...])
        ckv_ref[r, :] = ckv
        kpe = _rope(lat[:, off_pe:], *tab)
        kpe_ref[r, :] = kpe[:, :rope]
        if expand_kv:
            ckvn = ckv.astype(BF16)
            kpe_b = kpe.astype(BF16)
            for hp in range(N_HEADS // 2):
                kn = _dot(ckvn, wuk_ref[:, hp * HEAD_PAD:(hp + 1) * HEAD_PAD]).astype(BF16)
                base = off_k + 2 * hp * HEAD_PAD
                qkv_ref[r, base:base + LANE] = kn[:, :LANE]
                qkv_ref[r, base + LANE:base + HEAD_PAD] = kpe_b
                qkv_ref[r, base + HEAD_PAD:base + HEAD_PAD + LANE] = kn[:, LANE:]
                qkv_ref[r, base + HEAD_PAD + LANE:base + 2 * HEAD_PAD] = kpe_b
            qkv_ref[r, off_v:] = _dot(ckvn, wuv_ref[...]).astype(BF16)

    for r, h in zip(groups, hs):
        yg_ref[r, d:] = jax.nn.sigmoid(_dot(h, wg_ref[:, d:])).astype(BF16)

    us = [zc * _dot(h, wa_ref[:, d:off_q]) for zc, h in zip(zcs, hs)]
    w0 = convw_ref[0:1, :]
    w1 = convw_ref[1:2, :]
    w2 = convw_ref[2:3, :]
    rows = lax.broadcasted_iota(jnp.int32, (SUBLANE, 1), 0)
    ycs, tail = [], None
    for g, u in enumerate(us):
        parts = []
        pm = min(seg, group_rows[g])
        for p in range(group_rows[g] // pm):
            row0 = group_off[g] + p * pm
            sidx = row0 // seg
            up = u[p * pm:(p + 1) * pm]
            init = carry_ref[sidx] if row0 % seg == 0 else tail
            i0, i1 = init[0:1], init[1:2]
            r1, r2 = pltpu.roll(up, 1, 0), pltpu.roll(up, 2, 0)
            h1 = jnp.where(rows == 0, i1, r1[:SUBLANE])
            h2 = jnp.where(rows == 0, i0, jnp.where(rows == 1, i1, r2[:SUBLANE]))
            u1 = jnp.concatenate([h1, r1[SUBLANE:]], axis=0)
            u2 = jnp.concatenate([h2, r2[SUBLANE:]], axis=0)
            parts.append(w0 * u2 + w1 * u1 + w2 * up)
            tail = up[pm - (CONV_WIDTH - 1):pm]
            if (row0 + pm) % seg == 0:
                carry_ref[sidx] = tail
                cbuf_ref[sidx] = tail
        ycs.append(parts[0] if len(parts) == 1 else jnp.concatenate(parts, axis=0))

    gas = [_dot(h, wg_ref[:, :d]) for h in hs]
    zbs = [_dot(h, wb_ref[...]) for h in hs]
    for r, ga, zb, yc in zip(groups, gas, zbs, ycs):
        yg_ref[r, :d] = (jax.nn.sigmoid(ga) * zb * yc).astype(BF16)


def _proj_call(x2d, conv_init, tabs, wts, l, prev_stacks, *, n_outer, n_inner, tm, seg, dims,
               expand_kv):
    d, q_lora, kv_lora, rope, sm_scale = dims
    depth = wts[0].shape[0]
    nseg = tm // seg
    rows = x2d.shape[0]
    row_map = lambda o, i: (o * n_inner + i, 0)
    tab_map = lambda o, i: (i, 0)
    seg_map = lambda o, i: (o, 0, 0)
    lay_map = lambda o, i: (l, 0, 0)
    lrow_map = lambda o, i: (l, o * n_inner + i, 0)

    def wspec(a):
        return pl.BlockSpec((None,) + a.shape[1:], lay_map, pipeline_mode=pl.Buffered(1))

    def rspec(n):
        return pl.BlockSpec((tm, n), row_map)

    group_rows = tuple(tm * f // 8 for f in PROJ_GROUP_EIGHTHS)
    if sum(group_rows) != tm or any(z % (2 * SUBLANE) or (z % seg and seg % tm) for z in group_rows):
        group_rows = (tm,)
    kern = functools.partial(_proj_kernel, d=d, seg=seg, group_rows=group_rows, q_lora=q_lora,
                             kv_lora=kv_lora, rope=rope, sm_scale=sm_scale,
                             expand_kv=expand_kv, n_prev=len(prev_stacks))
    qkv_w = N_HEADS * (2 * HEAD_PAD + LANE) if expand_kv else N_HEADS * HEAD_PAD
    out_shape = [
        jax.ShapeDtypeStruct((rows, 2 * d), BF16),
        jax.ShapeDtypeStruct((rows, qkv_w), BF16),
        jax.ShapeDtypeStruct((depth, rows, kv_lora), F32),
        jax.ShapeDtypeStruct((depth, rows, rope), F32),
        jax.ShapeDtypeStruct((depth,) + conv_init.shape, F32),
    ]
    out_specs = [rspec(2 * d), rspec(qkv_w),
                 pl.BlockSpec((None, tm, kv_lora), lrow_map),
                 pl.BlockSpec((None, tm, rope), lrow_map),
                 pl.BlockSpec((None, nseg, CONV_WIDTH - 1, d), lambda o, i: (l, o, 0, 0))]
    in_specs = [rspec(d), pl.BlockSpec((nseg, CONV_WIDTH - 1, d), seg_map),
                pl.BlockSpec((tm, 3 * LANE), tab_map),
                *[wspec(w) for w in wts]]
    n_in = len(in_specs)
    in_specs += [pl.BlockSpec(memory_space=pl.ANY)] * len(prev_stacks)
    stack_out0 = 2
    return pl.pallas_call(
        kern,
        out_shape=out_shape,
        grid=(n_outer, n_inner),
        in_specs=in_specs,
        out_specs=out_specs,
        scratch_shapes=[pltpu.VMEM((nseg, CONV_WIDTH - 1, d), F32)],
        input_output_aliases={n_in + j: stack_out0 + j for j in range(len(prev_stacks))},
        compiler_params=pltpu.CompilerParams(
            dimension_semantics=("arbitrary", "arbitrary"),
            vmem_limit_bytes=VMEM_LIMIT_BYTES),
        name="proj",
    )(x2d, conv_init, tabs, *wts, *prev_stacks)


def _attn_prompt_kernel(q_ref, k_ref, v_ref, o_ref, *, tq, nq):
    i = pl.program_id(1)
    ncol = tq // LANE
    r_chunk = lax.broadcasted_iota(jnp.int32, (tq, LANE), 0) // CHUNK
    ones = jnp.ones((tq, LANE), BF16)

    def head_block(j, hh, masked, state):
        koff = j * tq
        q = q_ref[:, hh * HEAD_PAD:(hh + 1) * HEAD_PAD]
        k = k_ref[koff:koff + tq, hh * HEAD_PAD:(hh + 1) * HEAD_PAD]
        v = v_ref[koff:koff + tq, hh * LANE:(hh + 1) * LANE]
        s = _dot_nt(q, k)
        cols = []
        for c in range(ncol):
            sc = s[:, c * LANE:(c + 1) * LANE]
            if masked:
                c_chunk = (lax.broadcasted_iota(jnp.int32, (tq, LANE), 1) + c * LANE) // CHUNK
                sc = jnp.where(c_chunk <= r_chunk, sc, NEG_INF)
            cols.append(sc)
        m_cur = jnp.max(functools.reduce(jnp.maximum, cols), axis=1, keepdims=True)
        m_next = (jnp.broadcast_to(m_cur, (tq, LANE)) if state is None
                  else jnp.maximum(state[0], m_cur))
        p = jnp.concatenate([jnp.exp2(sc - m_next) for sc in cols], axis=1).astype(BF16)
        pv = _dot(p, jnp.concatenate([v, ones], axis=1))
        if state is None:
            return m_next, pv
        alpha = jnp.exp2(state[0] - m_next)
        return m_next, jnp.concatenate([alpha, alpha], axis=1) * state[1] + pv

    for qi in range(nq):
        @pl.when(i == qi)
        def _(qi=qi):
            state = [None] * N_HEADS
            for j in range(qi + 1):
                for hh in range(N_HEADS):
                    state[hh] = head_block(j, hh, j == qi, state[hh])
            for hh in range(N_HEADS):
                acc = state[hh][1]
                o_ref[:, hh * LANE:(hh + 1) * LANE] = (acc[:, :LANE] / acc[:, LANE:]).astype(BF16)


def _attn_prompt_call(qkv, *, b, s, tq):
    nq = s // tq
    wq = N_HEADS * HEAD_PAD
    wv = N_HEADS * LANE
    assert (2 * wq) % wv == 0
    return pl.pallas_call(
        functools.partial(_attn_prompt_kernel, tq=tq, nq=nq),
        out_shape=jax.ShapeDtypeStruct((b, s, N_HEADS * LANE), BF16),
        grid=(b, nq),
        in_specs=[pl.BlockSpec((None, tq, wq), lambda bb, i: (bb, i, 0)),
                  pl.BlockSpec((None, s, wq), lambda bb, i: (bb, 0, 1)),
                  pl.BlockSpec((None, s, wv), lambda bb, i: (bb, 0, 2 * wq // wv))],
        out_specs=pl.BlockSpec((None, tq, wv), lambda bb, i: (bb, i, 0)),
        compiler_params=pltpu.CompilerParams(
            dimension_semantics=("arbitrary", "arbitrary"),
            vmem_limit_bytes=VMEM_LIMIT_BYTES),
        name="attn_prompt",
    )(qkv, qkv, qkv)


def _attn_sample_kernel(q_ref, pckv_ref, pkpe_t_ref, nckv_ref, nkpe_ref, wuk_ref, wuv_ref,
                        o_ref, *, ls, rope, new_mask, nb):
    batches = range(nb)
    qas, qps = [], []
    for bi in batches:
        qa, qp = [], []
        for hh in range(N_HEADS):
            qn = q_ref[bi, :, hh * HEAD_PAD:hh * HEAD_PAD + LANE]
            qa.append(_dot_nt(qn, wuk_ref[:, hh * LANE:(hh + 1) * LANE]).astype(BF16))
            qp.append(q_ref[bi, :, hh * HEAD_PAD + LANE:hh * HEAD_PAD + LANE + rope])
        qas.append(jnp.concatenate(qa, axis=0))
        qps.append(jnp.concatenate(qp, axis=0))
    pckvs = [pckv_ref[bi].astype(BF16) for bi in batches]
    nckvs = [nckv_ref[bi].astype(BF16) for bi in batches]
    scores = []
    for bi in batches:
        s_past = _dot_nt(qas[bi], pckvs[bi]) + _dot(qps[bi], pkpe_t_ref[bi].astype(BF16))
        s_new = _dot_nt(qas[bi], nckvs[bi]) + _dot_nt(qps[bi], nkpe_ref[bi].astype(BF16))
        if new_mask is not None:
            r = lax.broadcasted_iota(jnp.int32, s_new.shape, 0) % ls
            c = lax.broadcasted_iota(jnp.int32, s_new.shape, 1)
            p0 = new_mask
            s_new = jnp.where((p0 + c) // CHUNK <= (p0 + r) // CHUNK, s_new, NEG_INF)
        scores.append((s_past, s_new))
    o_lats = []
    for bi, (s_past, s_new) in enumerate(scores):
        m = jnp.maximum(jnp.max(s_past, axis=1, keepdims=True),
                        jnp.max(s_new, axis=1, keepdims=True))
        p_past = jnp.exp2(s_past - m)
        p_new = jnp.exp2(s_new - m)
        denom = jnp.sum(p_past, axis=1, keepdims=True) + jnp.sum(p_new, axis=1, keepdims=True)
        o_lat = (_dot(p_past.astype(BF16), pckvs[bi]) + _dot(p_new.astype(BF16), nckvs[bi])) / denom
        o_lats.append(o_lat.astype(BF16))
    for bi, o_lat in enumerate(o_lats):
        for hh in range(N_HEADS):
            o_ref[bi, :, hh * LANE:(hh + 1) * LANE] = _dot(
                o_lat[hh * ls:(hh + 1) * ls], wuv_ref[:, hh * LANE:(hh + 1) * LANE]).astype(BF16)


def _attn_sample_call(q, cache_ckv, cache_kpe_t, nckv, nkpe, wuk, wuv, l, *, b, ls, new_mask):
    p, kv_lora = cache_ckv.shape[2], cache_ckv.shape[3]
    rope = cache_kpe_t.shape[2]
    nb = SAMPLE_BATCHES_PER_STEP if b % SAMPLE_BATCHES_PER_STEP == 0 else 1
    bmap = lambda bb: (bb, 0, 0)
    lmap = lambda bb: (l, bb, 0, 0)
    return pl.pallas_call(
        functools.partial(_attn_sample_kernel, ls=ls, rope=rope, new_mask=new_mask, nb=nb),
        out_shape=jax.ShapeDtypeStruct((b, ls, N_HEADS * LANE), BF16),
        grid=(b // nb,),
        in_specs=[pl.BlockSpec((nb, ls, N_HEADS * HEAD_PAD), bmap),
                  pl.BlockSpec((None, nb, p, kv_lora), lmap),
                  pl.BlockSpec((None, nb, rope, p), lmap),
                  pl.BlockSpec((None, nb, ls, kv_lora), lmap),
                  pl.BlockSpec((None, nb, ls, rope), lmap),
                  pl.BlockSpec((None,) + wuk.shape[1:], lambda bb: (l, 0, 0)),
                  pl.BlockSpec((None,) + wuv.shape[1:], lambda bb: (l, 0, 0))],
        out_specs=pl.BlockSpec((nb, ls, N_HEADS * LANE), bmap),
        compiler_params=pltpu.CompilerParams(
            dimension_semantics=("arbitrary",),
            vmem_limit_bytes=VMEM_LIMIT_BYTES),
        name="attn_sample",
    )(q, cache_ckv, cache_kpe_t, nckv, nkpe, wuk, wuv)


def _post_kernel(x_ref, yg_ref, yb_ref, wo_ref, w1_ref, w2_ref,
                 gpm_ref, gpf_ref, gqf_ref, o_ref, *, d_ff, ff_chunk, group_rows):
    tm, d = x_ref.shape
    groups = [pl.ds(sum(group_rows[:g]), z) for g, z in enumerate(group_rows)]
    n_chunks = d_ff // ff_chunk
    up = lambda h2, c: _dot(h2, w1_ref[:, c * ff_chunk:(c + 1) * ff_chunk])

    outs = []
    for r in groups:
        merged = yg_ref[r, :d].astype(F32) + yg_ref[r, d:].astype(F32) * yb_ref[r, :].astype(F32)
        outs.append(_dot(merged.astype(BF16), wo_ref[...]))
    x1s = [x_ref[r, :] + _rms(o, gpm_ref[...]) for r, o in zip(groups, outs)]
    h2s = [_rms(x1, gpf_ref[...]).astype(BF16) for x1 in x1s]
    a_next = [up(h2, 0) for h2 in h2s]
    accs = [jnp.zeros(x1.shape, F32) for x1 in x1s]
    for c in range(n_chunks):
        for i, h2 in enumerate(h2s):
            a = jnp.square(jnp.maximum(a_next[i], 0.0)).astype(BF16)
            if c + 1 < n_chunks:
                a_next[i] = up(h2, c + 1)
            accs[i] = accs[i] + _dot(a, w2_ref[c * ff_chunk:(c + 1) * ff_chunk, :])
    for r, x1, acc in zip(groups, x1s, accs):
        o_ref[r, :] = x1 + _rms(acc, gqf_ref[...])


def _post_call(x2d, yg, yb, wts, l, *, tm):
    rows, d = x2d.shape
    wo, w1, w2, gpm, gpf, gqf = wts
    d_ff = w1.shape[-1]
    row_map = lambda i: (i, 0)
    lay_map = lambda i: (l, 0, 0)

    def wspec(a):
        return pl.BlockSpec((None,) + a.shape[1:], lay_map, pipeline_mode=pl.Buffered(1))

    rspec = pl.BlockSpec((tm, d), row_map)
    group_rows = tuple(tm * f // 8 for f in POST_GROUP_EIGHTHS)
    if sum(group_rows) != tm or any(z % (2 * SUBLANE) for z in group_rows):
        group_rows = (tm,)
    return pl.pallas_call(
        functools.partial(_post_kernel, d_ff=d_ff, ff_chunk=min(d_ff, 1024),
                          group_rows=group_rows),
        out_shape=jax.ShapeDtypeStruct((rows, d), F32),
        grid=(rows // tm,),
        in_specs=[rspec, pl.BlockSpec((tm, 2 * d), row_map), rspec, wspec(wo), wspec(w1), wspec(w2),
                  wspec(gpm), wspec(gpf), wspec(gqf)],
        out_specs=rspec,
        compiler_params=pltpu.CompilerParams(
            dimension_semantics=("arbitrary",),
            vmem_limit_bytes=VMEM_LIMIT_BYTES),
        name="post",
    )(x2d, yg, yb, wo, w1, w2, gpm, gpf, gqf)


def _rope_tables(pos, rope):
    inv = ROPE_THETA ** (-jnp.arange(0, rope, 2, dtype=F32) / rope)
    ang = pos.astype(F32)[:, None] * inv[None, :]
    cos, sin = jnp.cos(ang), jnp.sin(ang)
    z = jnp.zeros_like(cos)
    pad = jnp.zeros((pos.shape[0], LANE - rope), F32)
    cos_t = jnp.concatenate([cos, cos, pad], axis=1)
    sin_a = jnp.concatenate([z, sin, pad], axis=1)
    sin_b = jnp.concatenate([-sin, z, pad], axis=1)
    return jnp.concatenate([cos_t, sin_a, sin_b], axis=1)


def _prep_weights(w_in, conv_w, w_uq, w_uk, w_uv, w_o, w1, w2, g_pre_mix, g_post_mix, g_pre_ffn,
                  g_post_ffn, g_q, g_kv, *, nope, rope):
    depth, d, _ = w_in.shape
    q_lora = w_uq.shape[1]
    off_ga = w_in.shape[-1] - 2 * d
    w_a = jnp.pad(w_in[:, :, d:off_ga],
                  ((0, 0), (0, 0), (0, LANE - rope))).astype(BF16)
    w_b = w_in[:, :, :d].astype(BF16)
    w_g = w_in[:, :, off_ga:].astype(BF16)
    uq = w_uq.reshape(depth, q_lora, N_HEADS, nope + rope)
    wuq = jnp.concatenate(
        [uq[..., :nope].reshape(depth, q_lora, N_HEADS * nope),
         uq[..., nope:].reshape(depth, q_lora, N_HEADS * rope)], axis=2).astype(BF16)
    wuk, wuv = w_uk.astype(BF16), w_uv.astype(BF16)
    g3 = lambda g: g[:, None, :]
    proj_w = (w_a, w_b, w_g, wuq, wuk, wuv, conv_w, g3(g_pre_mix), g3(g_q), g3(g_kv))
    post_w = (w_o.astype(BF16), w1.astype(BF16), w2.astype(BF16),
              g3(g_post_mix), g3(g_pre_ffn), g3(g_post_ffn))
    return proj_w, post_w, wuk, wuv


def _tile(n, pref):
    t = min(n, pref)
    assert n % t == 0, (n, t)
    return t


def kernel(x_prompt, x_sample, cache_ckv, cache_kpe, state_conv, w_in, conv_w, w_uq, w_uk, w_uv,
           w_o, w1, w2, g_pre_mix, g_post_mix, g_pre_ffn, g_post_ffn, g_q, g_kv):
    b, s, d = x_prompt.shape
    bs, ls, _ = x_sample.shape
    depth, _, p, kv_lora = cache_ckv.shape
    rope = cache_kpe.shape[-1]
    q_lora = g_q.shape[-1]
    nope = w_uk.shape[-1] // N_HEADS
    v_dim = w_uv.shape[-1] // N_HEADS
    assert nope == LANE and v_dim == LANE and rope * 2 == LANE and d == N_HEADS * v_dim
    assert conv_w.shape[1] == CONV_WIDTH and s % CHUNK == 0 and ls >= CONV_WIDTH - 1
    q_scale = float(nope + rope) ** -0.5 * float(np.log2(np.e))
    dims = (d, q_lora, kv_lora, rope, q_scale)

    assert q_lora % LANE == 0 and kv_lora % LANE == 0
    proj_w, post_w, wuk, wuv = _prep_weights(
        w_in, conv_w, w_uq, w_uk, w_uv, w_o, w1, w2, g_pre_mix, g_post_mix, g_pre_ffn,
        g_post_ffn, g_q, g_kv, nope=nope, rope=rope)
    cache_kpe_t = jnp.swapaxes(cache_kpe, 2, 3)

    tm_p = _tile(s, ROW_TILE)
    tq = _tile(s, ROW_TILE)
    segs_per_tile = max(1, min(bs, ROW_TILE // ls))
    assert bs % segs_per_tile == 0
    tm_s = segs_per_tile * ls
    tabs_p = _rope_tables(jnp.arange(s, dtype=jnp.int32), rope)
    tabs_s = jnp.tile(_rope_tables(p + jnp.arange(ls, dtype=jnp.int32), rope), (segs_per_tile, 1))
    pos_new = p + np.arange(ls)
    new_all_visible = bool(np.all((pos_new[None, :] // CHUNK) <= (pos_new[:, None] // CHUNK)))
    new_mask = None if new_all_visible else p

    xp = x_prompt.reshape(b * s, d)
    xs = x_sample.reshape(bs * ls, d)
    zero_buf = jnp.zeros((b, CONV_WIDTH - 1, d), F32)
    stk_p, stk_s = (), ()
    for l in range(depth):
        yg, qkv, *stk_p = _proj_call(
            xp, zero_buf, tabs_p, proj_w, l, stk_p, n_outer=b, n_inner=s // tm_p, tm=tm_p,
            seg=tm_p, dims=dims, expand_kv=True)
        yb = _attn_prompt_call(qkv.reshape(b, s, -1), b=b, s=s, tq=tq)
        xp = _post_call(xp, yg, yb.reshape(b * s, d), post_w, l, tm=tm_p)

        yg, q, *stk_s = _proj_call(
            xs, state_conv[l], tabs_s, proj_w, l, stk_s, n_outer=bs // segs_per_tile, n_inner=1,
            tm=tm_s, seg=ls, dims=dims, expand_kv=False)
        yb = _attn_sample_call(q.reshape(bs, ls, -1), cache_ckv, cache_kpe_t,
                               stk_s[0].reshape(depth, bs, ls, kv_lora),
                               stk_s[1].reshape(depth, bs, ls, rope),
                               wuk, wuv, l, b=bs, ls=ls, new_mask=new_mask)
        xs = _post_call(xs, yg, yb.reshape(bs * ls, d), post_w, l, tm=tm_s)

    return (xp.reshape(b, s, d), xs.reshape(bs, ls, d),
            stk_p[0].reshape(depth, b, s, kv_lora), stk_p[1].reshape(depth, b, s, rope), stk_p[2],
            stk_s[0].reshape(depth, bs, ls, kv_lora), stk_s[1].reshape(depth, bs, ls, rope),
            stk_s[2])
```

```python
import functools

import numpy as np
import jax
import jax.numpy as jnp
from jax import lax
from jax.experimental import pallas as pl
from jax.experimental.pallas import tpu as pltpu

CHUNK = 64
CONV_WIDTH = 3
N_HEADS = 8
ROPE_THETA = 10000.0
EPS = 1e-6
NEG_INF = -1e30

LANE = 128
SUBLANE = 8
HEAD_PAD = 2 * LANE
VMEM_LIMIT_BYTES = 56 * 1024 * 1024
ROW_TILE = 512
POST_GROUP_EIGHTHS = (5, 3)
PROJ_GROUP_EIGHTHS = (3, 5)
SAMPLE_BATCHES_PER_STEP = 4

F32 = jnp.float32
BF16 = jnp.bfloat16


def _rms(x, g):
    ms = jnp.mean(x * x, axis=-1, keepdims=True)
    return x * lax.rsqrt(ms + EPS) * g


def _rope(pe, cos_t, sin_a, sin_b):
    return (pe * cos_t + pltpu.roll(pe, LANE // 4, 1) * sin_a
            + pltpu.roll(pe, LANE - LANE // 4, 1) * sin_b)


def _dot(a, b):
    return jnp.dot(a, b, preferred_element_type=F32)


def _dot_nt(a, b):
    return lax.dot_general(a, b, (((1,), (1,)), ((), ())), preferred_element_type=F32)


def _proj_kernel(x_ref, init_ref, tab_ref,
                 wa_ref, wb_ref, wg_ref, wuq_ref, wuk_ref, wuv_ref, convw_ref,
                 gpre_ref, gq_ref, gkv_ref, *rest,
                 d, seg, group_rows, q_lora, kv_lora, rope, sm_scale, expand_kv, n_prev):
    yg_ref, qkv_ref, ckv_ref, kpe_ref, cbuf_ref, carry_ref = rest[n_prev:]
    off_k = N_HEADS * HEAD_PAD
    off_v = 2 * off_k
    inner = pl.program_id(1)
    group_off = [sum(group_rows[:g]) for g in range(len(group_rows))]
    groups = [pl.ds(o, z) for o, z in zip(group_off, group_rows)]
    off_q = 2 * d
    off_kv = q_lora
    off_pe = off_kv + kv_lora

    @pl.when(inner == 0)
    def _():
        carry_ref[...] = init_ref[...]

    hs = [_rms(x_ref[r, :], gpre_ref[...]).astype(BF16) for r in groups]
    tabs = [tuple(tab_ref[r, t * LANE:(t + 1) * LANE] for t in range(3)) for r in groups]
    lats = [_dot(h, wa_ref[:, off_q:]) for h in hs]
    zcs = [_dot(h, wa_ref[:, :d]) for h in hs]

    for r, lat, tab in zip(groups, lats, tabs):
        cqn = _rms(lat[:, :off_kv], gq_ref[...]).astype(BF16)
        qn = _dot(cqn, wuq_ref[:, :N_HEADS * LANE]) * sm_scale
        qp = _dot(cqn, wuq_ref[:, N_HEADS * LANE:]) * sm_scale
        for hh in range(N_HEADS):
            pe = qp[:, (hh // 2) * LANE:(hh // 2 + 1) * LANE]
            if hh % 2:
                pe = pltpu.roll(pe, LANE // 2, 1)
            qkv_ref[r, hh * HEAD_PAD:hh * HEAD_PAD + LANE] = qn[:, hh * LANE:(hh + 1) * LANE].astype(BF16)
            qkv_ref[r, hh * HEAD_PAD + LANE:(hh + 1) * HEAD_PAD] = _rope(pe, *tab).astype(BF16)

    for r, lat, tab in zip(groups, lats, tabs):
        ckv = _rms(lat[:, off_kv:off_pe], gkv_ref[...])
        ckv_ref[r, :] = ckv
        kpe = _rope(lat[:, off_pe:], *tab)
        kpe_ref[r, :] = kpe[:, :rope]
        if expand_kv:
            ckvn = ckv.astype(BF16)
            kpe_b = kpe.astype(BF16)
            for hp in range(N_HEADS // 2):
                kn = _dot(ckvn, wuk_ref[:, hp * HEAD_PAD:(hp + 1) * HEAD_PAD]).astype(BF16)
                base = off_k + 2 * hp * HEAD_PAD
                qkv_ref[r, base:base + LANE] = kn[:, :LANE]
                qkv_ref[r, base + LANE:base + HEAD_PAD] = kpe_b
                qkv_ref[r, base + HEAD_PAD:base + HEAD_PAD + LANE] = kn[:, LANE:]
                qkv_ref[r, base + HEAD_PAD + LANE:base + 2 * HEAD_PAD] = kpe_b
            qkv_ref[r, off_v:] = _dot(ckvn, wuv_ref[...]).astype(BF16)

    for r, h in zip(groups, hs):
        yg_ref[r, d:] = jax.nn.sigmoid(_dot(h, wg_ref[:, d:])).astype(BF16)

    us = [zc * _dot(h, wa_ref[:, d:off_q]) for zc, h in zip(zcs, hs)]
    w0 = convw_ref[0:1, :]
    w1 = convw_ref[1:2, :]
    w2 = convw_ref[2:3, :]
    rows = lax.broadcasted_iota(jnp.int32, (SUBLANE, 1), 0)
    ycs, tail = [], None
    for g, u in enumerate(us):
        parts = []
        pm = min(seg, group_rows[g])
        for p in range(group_rows[g] // pm):
            row0 = group_off[g] + p * pm
            sidx = row0 // seg
            up = u[p * pm:(p + 1) * pm]
            init = carry_ref[sidx] if row0 % seg == 0 else tail
            i0, i1 = init[0:1], init[1:2]
            r1, r2 = pltpu.roll(up, 1, 0), pltpu.roll(up, 2, 0)
            h1 = jnp.where(rows == 0, i1, r1[:SUBLANE])
            h2 = jnp.where(rows == 0, i0, jnp.where(rows == 1, i1, r2[:SUBLANE]))
            u1 = jnp.concatenate([h1, r1[SUBLANE:]], axis=0)
            u2 = jnp.concatenate([h2, r2[SUBLANE:]], axis=0)
            parts.append(w0 * u2 + w1 * u1 + w2 * up)
            tail = up[pm - (CONV_WIDTH - 1):pm]
            if (row0 + pm) % seg == 0:
                carry_ref[sidx] = tail
                cbuf_ref[sidx] = tail
        ycs.append(parts[0] if len(parts) == 1 else jnp.concatenate(parts, axis=0))

    gas = [_dot(h, wg_ref[:, :d]) for h in hs]
    zbs = [_dot(h, wb_ref[...]) for h in hs]
    for r, ga, zb, yc in zip(groups, gas, zbs, ycs):
        yg_ref[r, :d] = (jax.nn.sigmoid(ga) * zb * yc).astype(BF16)


def _proj_call(x2d, conv_init, tabs, wts, l, prev_stacks, *, n_outer, n_inner, tm, seg, dims,
               expand_kv):
    d, q_lora, kv_lora, rope, sm_scale = dims
    depth = wts[0].shape[0]
    nseg = tm // seg
    rows = x2d.shape[0]
    row_map = lambda o, i: (o * n_inner + i, 0)
    tab_map = lambda o, i: (i, 0)
    seg_map = lambda o, i: (o, 0, 0)
    lay_map = lambda o, i: (l, 0, 0)
    lrow_map = lambda o, i: (l, o * n_inner + i, 0)

    def wspec(a):
        return pl.BlockSpec((None,) + a.shape[1:], lay_map, pipeline_mode=pl.Buffered(1))

    def rspec(n):
        return pl.BlockSpec((tm, n), row_map)

    group_rows = tuple(tm * f // 8 for f in PROJ_GROUP_EIGHTHS)
    if sum(group_rows) != tm or any(z % (2 * SUBLANE) or (z % seg and seg % tm) for z in group_rows):
        group_rows = (tm,)
    kern = functools.partial(_proj_kernel, d=d, seg=seg, group_rows=group_rows, q_lora=q_lora,
                             kv_lora=kv_lora, rope=rope, sm_scale=sm_scale,
                             expand_kv=expand_kv, n_prev=len(prev_stacks))
    qkv_w = N_HEADS * (2 * HEAD_PAD + LANE) if expand_kv else N_HEADS * HEAD_PAD
    out_shape = [
        jax.ShapeDtypeStruct((rows, 2 * d), BF16),
        jax.ShapeDtypeStruct((rows, qkv_w), BF16),
        jax.ShapeDtypeStruct((depth, rows, kv_lora), F32),
        jax.ShapeDtypeStruct((depth, rows, rope), F32),
        jax.ShapeDtypeStruct((depth,) + conv_init.shape, F32),
    ]
    out_specs = [rspec(2 * d), rspec(qkv_w),
                 pl.BlockSpec((None, tm, kv_lora), lrow_map),
                 pl.BlockSpec((None, tm, rope), lrow_map),
                 pl.BlockSpec((None, nseg, CONV_WIDTH - 1, d), lambda o, i: (l, o, 0, 0))]
    in_specs = [rspec(d), pl.BlockSpec((nseg, CONV_WIDTH - 1, d), seg_map),
                pl.BlockSpec((tm, 3 * LANE), tab_map),
                *[wspec(w) for w in wts]]
    n_in = len(in_specs)
    in_specs += [pl.BlockSpec(memory_space=pl.ANY)] * len(prev_stacks)
    stack_out0 = 2
    return pl.pallas_call(
        kern,
        out_shape=out_shape,
        grid=(n_outer, n_inner),
        in_specs=in_specs,
        out_specs=out_specs,
        scratch_shapes=[pltpu.VMEM((nseg, CONV_WIDTH - 1, d), F32)],
        input_output_aliases={n_in + j: stack_out0 + j for j in range(len(prev_stacks))},
        compiler_params=pltpu.CompilerParams(
            dimension_semantics=("arbitrary", "arbitrary"),
            vmem_limit_bytes=VMEM_LIMIT_BYTES),
        name="proj",
    )(x2d, conv_init, tabs, *wts, *prev_stacks)


def _attn_prompt_kernel(q_ref, k_ref, v_ref, o_ref, *, tq, nq):
    i = pl.program_id(1)
    ncol = tq // LANE
    r_chunk = lax.broadcasted_iota(jnp.int32, (tq, LANE), 0) // CHUNK
    ones = jnp.ones((tq, LANE), BF16)

    def head_block(j, hh, masked, state):
        koff = j * tq
        q = q_ref[:, hh * HEAD_PAD:(hh + 1) * HEAD_PAD]
        k = k_ref[koff:koff + tq, hh * HEAD_PAD:(hh + 1) * HEAD_PAD]
        v = v_ref[koff:koff + tq, hh * LANE:(hh + 1) * LANE]
        s = _dot_nt(q, k)
        cols = []
        for c in range(ncol):
            sc = s[:, c * LANE:(c + 1) * LANE]
            if masked:
                c_chunk = (lax.broadcasted_iota(jnp.int32, (tq, LANE), 1) + c * LANE) // CHUNK
                sc = jnp.where(c_chunk <= r_chunk, sc, NEG_INF)
            cols.append(sc)
        m_cur = jnp.max(functools.reduce(jnp.maximum, cols), axis=1, keepdims=True)
        m_next = (jnp.broadcast_to(m_cur, (tq, LANE)) if state is None
                  else jnp.maximum(state[0], m_cur))
        p = jnp.concatenate([jnp.exp2(sc - m_next) for sc in cols], axis=1).astype(BF16)
        pv = _dot(p, jnp.concatenate([v, ones], axis=1))
        if state is None:
            return m_next, pv
        alpha = jnp.exp2(state[0] - m_next)
        return m_next, jnp.concatenate([alpha, alpha], axis=1) * state[1] + pv

    for qi in range(nq):
        @pl.when(i == qi)
        def _(qi=qi):
            state = [None] * N_HEADS
            for j in range(qi + 1):
                for hh in range(N_HEADS):
                    state[hh] = head_block(j, hh, j == qi, state[hh])
            for hh in range(N_HEADS):
                acc = state[hh][1]
                o_ref[:, hh * LANE:(hh + 1) * LANE] = (acc[:, :LANE] / acc[:, LANE:]).astype(BF16)


def _attn_prompt_call(qkv, *, b, s, tq):
    nq = s // tq
    wq = N_HEADS * HEAD_PAD
    wv = N_HEADS * LANE
    assert (2 * wq) % wv == 0
    return pl.pallas_call(
        functools.partial(_attn_prompt_kernel, tq=tq, nq=nq),
        out_shape=jax.ShapeDtypeStruct((b, s, N_HEADS * LANE), BF16),
        grid=(b, nq),
        in_specs=[pl.BlockSpec((None, tq, wq), lambda bb, i: (bb, i, 0)),
                  pl.BlockSpec((None, s, wq), lambda bb, i: (bb, 0, 1)),
                  pl.BlockSpec((None, s, wv), lambda bb, i: (bb, 0, 2 * wq // wv))],
        out_specs=pl.BlockSpec((None, tq, wv), lambda bb, i: (bb, i, 0)),
        compiler_params=pltpu.CompilerParams(
            dimension_semantics=("arbitrary", "arbitrary"),
            vmem_limit_bytes=VMEM_LIMIT_BYTES),
        name="attn_prompt",
    )(qkv, qkv, qkv)


def _attn_sample_kernel(q_ref, pckv_ref, pkpe_t_ref, nckv_ref, nkpe_ref, wuk_ref, wuv_ref,
                        o_ref, *, ls, rope, new_mask, nb):
    batches = range(nb)
    qas, qps = [], []
    for bi in batches:
        qa, qp = [], []
        for hh in range(N_HEADS):
            qn = q_ref[bi, :, hh * HEAD_PAD:hh * HEAD_PAD + LANE]
            qa.append(_dot_nt(qn, wuk_ref[:, hh * LANE:(hh + 1) * LANE]).astype(BF16))
            qp.append(q_ref[bi, :, hh * HEAD_PAD + LANE:hh * HEAD_PAD + LANE + rope])
        qas.append(jnp.concatenate(qa, axis=0))
        qps.append(jnp.concatenate(qp, axis=0))
    pckvs = [pckv_ref[bi].astype(BF16) for bi in batches]
    nckvs = [nckv_ref[bi].astype(BF16) for bi in batches]
    scores = []
    for bi in batches:
        s_past = _dot_nt(qas[bi], pckvs[bi]) + _dot(qps[bi], pkpe_t_ref[bi].astype(BF16))
        s_new = _dot_nt(qas[bi], nckvs[bi]) + _dot_nt(qps[bi], nkpe_ref[bi].astype(BF16))
        if new_mask is not None:
            r = lax.broadcasted_iota(jnp.int32, s_new.shape, 0) % ls
            c = lax.broadcasted_iota(jnp.int32, s_new.shape, 1)
            p0 = new_mask
            s_new = jnp.where((p0 + c) // CHUNK <= (p0 + r) // CHUNK, s_new, NEG_INF)
        scores.append((s_past, s_new))
    o_lats = []
    for bi, (s_past, s_new) in enumerate(scores):
        m = jnp.maximum(jnp.max(s_past, axis=1, keepdims=True),
                        jnp.max(s_new, axis=1, keepdims=True))
        p_past = jnp.exp2(s_past - m)
        p_new = jnp.exp2(s_new - m)
        denom = jnp.sum(p_past, axis=1, keepdims=True) + jnp.sum(p_new, axis=1, keepdims=True)
        o_lat = (_dot(p_past.astype(BF16), pckvs[bi]) + _dot(p_new.astype(BF16), nckvs[bi])) / denom
        o_lats.append(o_lat.astype(BF16))
    for bi, o_lat in enumerate(o_lats):
        for hh in range(N_HEADS):
            o_ref[bi, :, hh * LANE:(hh + 1) * LANE] = _dot(
                o_lat[hh * ls:(hh + 1) * ls], wuv_ref[:, hh * LANE:(hh + 1) * LANE]).astype(BF16)


def _attn_sample_call(q, cache_ckv, cache_kpe_t, nckv, nkpe, wuk, wuv, l, *, b, ls, new_mask):
    p, kv_lora = cache_ckv.shape[2], cache_ckv.shape[3]
    rope = cache_kpe_t.shape[2]
    nb = SAMPLE_BATCHES_PER_STEP if b % SAMPLE_BATCHES_PER_STEP == 0 else 1
    bmap = lambda bb: (bb, 0, 0)
    lmap = lambda bb: (l, bb, 0, 0)
    return pl.pallas_call(
        functools.partial(_attn_sample_kernel, ls=ls, rope=rope, new_mask=new_mask, nb=nb),
        out_shape=jax.ShapeDtypeStruct((b, ls, N_HEADS * LANE), BF16),
        grid=(b // nb,),
        in_specs=[pl.BlockSpec((nb, ls, N_HEADS * HEAD_PAD), bmap),
                  pl.BlockSpec((None, nb, p, kv_lora), lmap),
                  pl.BlockSpec((None, nb, rope, p), lmap),
                  pl.BlockSpec((None, nb, ls, kv_lora), lmap),
                  pl.BlockSpec((None, nb, ls, rope), lmap),
                  pl.BlockSpec((None,) + wuk.shape[1:], lambda bb: (l, 0, 0)),
                  pl.BlockSpec((None,) + wuv.shape[1:], lambda bb: (l, 0, 0))],
        out_specs=pl.BlockSpec((nb, ls, N_HEADS * LANE), bmap),
        compiler_params=pltpu.CompilerParams(
            dimension_semantics=("arbitrary",),
            vmem_limit_bytes=VMEM_LIMIT_BYTES),
        name="attn_sample",
    )(q, cache_ckv, cache_kpe_t, nckv, nkpe, wuk, wuv)


def _post_kernel(x_ref, yg_ref, yb_ref, wo_ref, w1_ref, w2_ref,
                 gpm_ref, gpf_ref, gqf_ref, o_ref, *, d_ff, ff_chunk, group_rows):
    tm, d = x_ref.shape
    groups = [pl.ds(sum(group_rows[:g]), z) for g, z in enumerate(group_rows)]
    n_chunks = d_ff // ff_chunk
    up = lambda h2, c: _dot(h2, w1_ref[:, c * ff_chunk:(c + 1) * ff_chunk])

    outs = []
    for r in groups:
        merged = yg_ref[r, :d].astype(F32) + yg_ref[r, d:].astype(F32) * yb_ref[r, :].astype(F32)
        outs.append(_dot(merged.astype(BF16), wo_ref[...]))
    x1s = [x_ref[r, :] + _rms(o, gpm_ref[...]) for r, o in zip(groups, outs)]
    h2s = [_rms(x1, gpf_ref[...]).astype(BF16) for x1 in x1s]
    a_next = [up(h2, 0) for h2 in h2s]
    accs = [jnp.zeros(x1.shape, F32) for x1 in x1s]
    for c in range(n_chunks):
        for i, h2 in enumerate(h2s):
            a = jnp.square(jnp.maximum(a_next[i], 0.0)).astype(BF16)
            if c + 1 < n_chunks:
                a_next[i] = up(h2, c + 1)
            accs[i] = accs[i] + _dot(a, w2_ref[c * ff_chunk:(c + 1) * ff_chunk, :])
    for r, x1, acc in zip(groups, x1s, accs):
        o_ref[r, :] = x1 + _rms(acc, gqf_ref[...])


def _post_call(x2d, yg, yb, wts, l, *, tm):
    rows, d = x2d.shape
    wo, w1, w2, gpm, gpf, gqf = wts
    d_ff = w1.shape[-1]
    row_map = lambda i: (i, 0)
    lay_map = lambda i: (l, 0, 0)

    def wspec(a):
        return pl.BlockSpec((None,) + a.shape[1:], lay_map, pipeline_mode=pl.Buffered(1))

    rspec = pl.BlockSpec((tm, d), row_map)
    group_rows = tuple(tm * f // 8 for f in POST_GROUP_EIGHTHS)
    if sum(group_rows) != tm or any(z % (2 * SUBLANE) for z in group_rows):
        group_rows = (tm,)
    return pl.pallas_call(
        functools.partial(_post_kernel, d_ff=d_ff, ff_chunk=min(d_ff, 1024),
                          group_rows=group_rows),
        out_shape=jax.ShapeDtypeStruct((rows, d), F32),
        grid=(rows // tm,),
        in_specs=[rspec, pl.BlockSpec((tm, 2 * d), row_map), rspec, wspec(wo), wspec(w1), wspec(w2),
                  wspec(gpm), wspec(gpf), wspec(gqf)],
        out_specs=rspec,
        compiler_params=pltpu.CompilerParams(
            dimension_semantics=("arbitrary",),
            allow_input_fusion=[False] * 3 + [True] * 3 + [False] * 3,
            vmem_limit_bytes=VMEM_LIMIT_BYTES),
        name="post",
    )(x2d, yg, yb, wo, w1, w2, gpm, gpf, gqf)


def _rope_tables(pos, rope):
    inv = ROPE_THETA ** (-jnp.arange(0, rope, 2, dtype=F32) / rope)
    ang = pos.astype(F32)[:, None] * inv[None, :]
    cos, sin = jnp.cos(ang), jnp.sin(ang)
    z = jnp.zeros_like(cos)
    pad = jnp.zeros((pos.shape[0], LANE - rope), F32)
    cos_t = jnp.concatenate([cos, cos, pad], axis=1)
    sin_a = jnp.concatenate([z, sin, pad], axis=1)
    sin_b = jnp.concatenate([-sin, z, pad], axis=1)
    return jnp.concatenate([cos_t, sin_a, sin_b], axis=1)


def _prep_weights(w_in, conv_w, w_uq, w_uk, w_uv, w_o, w1, w2, g_pre_mix, g_post_mix, g_pre_ffn,
                  g_post_ffn, g_q, g_kv, *, nope, rope):
    depth, d, _ = w_in.shape
    q_lora = w_uq.shape[1]
    off_ga = w_in.shape[-1] - 2 * d
    w_a = jnp.pad(w_in[:, :, d:off_ga],
                  ((0, 0), (0, 0), (0, LANE - rope))).astype(BF16)
    w_b = w_in[:, :, :d].astype(BF16)
    w_g = w_in[:, :, off_ga:].astype(BF16)
    uq = w_uq.reshape(depth, q_lora, N_HEADS, nope + rope)
    wuq = jnp.concatenate(
        [uq[..., :nope].reshape(depth, q_lora, N_HEADS * nope),
         uq[..., nope:].reshape(depth, q_lora, N_HEADS * rope)], axis=2).astype(BF16)
    wuk, wuv = w_uk.astype(BF16), w_uv.astype(BF16)
    g3 = lambda g: g[:, None, :]
    proj_w = (w_a, w_b, w_g, wuq, wuk, wuv, conv_w, g3(g_pre_mix), g3(g_q), g3(g_kv))
    post_w = (w_o.astype(BF16), w1.astype(BF16), w2.astype(BF16),
              g3(g_post_mix), g3(g_pre_ffn), g3(g_post_ffn))
    return proj_w, post_w, wuk, wuv


def _tile(n, pref):
    t = min(n, pref)
    assert n % t == 0, (n, t)
    return t


def kernel(x_prompt, x_sample, cache_ckv, cache_kpe, state_conv, w_in, conv_w, w_uq, w_uk, w_uv,
           w_o, w1, w2, g_pre_mix, g_post_mix, g_pre_ffn, g_post_ffn, g_q, g_kv):
    b, s, d = x_prompt.shape
    bs, ls, _ = x_sample.shape
    depth, _, p, kv_lora = cache_ckv.shape
    rope = cache_kpe.shape[-1]
    q_lora = g_q.shape[-1]
    nope = w_uk.shape[-1] // N_HEADS
    v_dim = w_uv.shape[-1] // N_HEADS
    assert nope == LANE and v_dim == LANE and rope * 2 == LANE and d == N_HEADS * v_dim
    assert conv_w.shape[1] == CONV_WIDTH and s % CHUNK == 0 and ls >= CONV_WIDTH - 1
    q_scale = float(nope + rope) ** -0.5 * float(np.log2(np.e))
    dims = (d, q_lora, kv_lora, rope, q_scale)

    assert q_lora % LANE == 0 and kv_lora % LANE == 0
    proj_w, post_w, wuk, wuv = _prep_weights(
        w_in, conv_w, w_uq, w_uk, w_uv, w_o, w1, w2, g_pre_mix, g_post_mix, g_pre_ffn,
        g_post_ffn, g_q, g_kv, nope=nope, rope=rope)
    cache_kpe_t = jnp.swapaxes(cache_kpe, 2, 3)

    tm_p = _tile(s, ROW_TILE)
    tq = _tile(s, ROW_TILE)
    segs_per_tile = max(1, min(bs, ROW_TILE // ls))
    assert bs % segs_per_tile == 0
    tm_s = segs_per_tile * ls
    tabs_p = _rope_tables(jnp.arange(s, dtype=jnp.int32), rope)
    tabs_s = jnp.tile(_rope_tables(p + jnp.arange(ls, dtype=jnp.int32), rope), (segs_per_tile, 1))
    pos_new = p + np.arange(ls)
    new_all_visible = bool(np.all((pos_new[None, :] // CHUNK) <= (pos_new[:, None] // CHUNK)))
    new_mask = None if new_all_visible else p

    xp = x_prompt.reshape(b * s, d)
    xs = x_sample.reshape(bs * ls, d)
    zero_buf = jnp.zeros((b, CONV_WIDTH - 1, d), F32)
    stk_p, stk_s = (), ()
    for l in range(depth):
        yg, qkv, *stk_p = _proj_call(
            xp, zero_buf, tabs_p, proj_w, l, stk_p, n_outer=b, n_inner=s // tm_p, tm=tm_p,
            seg=tm_p, dims=dims, expand_kv=True)
        yb = _attn_prompt_call(qkv.reshape(b, s, -1), b=b, s=s, tq=tq)
        xp = _post_call(xp, yg, yb.reshape(b * s, d), post_w, l, tm=tm_p)

        yg, q, *stk_s = _proj_call(
            xs, state_conv[l], tabs_s, proj_w, l, stk_s, n_outer=bs // segs_per_tile, n_inner=1,
            tm=tm_s, seg=ls, dims=dims, expand_kv=False)
        yb = _attn_sample_call(q.reshape(bs, ls, -1), cache_ckv, cache_kpe_t,
                               stk_s[0].reshape(depth, bs, ls, kv_lora),
                               stk_s[1].reshape(depth, bs, ls, rope),
                               wuk, wuv, l, b=bs, ls=ls, new_mask=new_mask)
        xs = _post_call(xs, yg, yb.reshape(bs * ls, d), post_w, l, tm=tm_s)

    return (xp.reshape(b, s, d), xs.reshape(bs, ls, d),
            stk_p[0].reshape(depth, b, s, kv_lora), stk_p[1].reshape(depth, b, s, rope), stk_p[2],
            stk_s[0].reshape(depth, bs, ls, kv_lora), stk_s[1].reshape(depth, bs, ls, rope),
            stk_s[2])
```
